```python
import jax, jax.numpy as jnp
from jax import lax
import numpy as np

D_MODEL = 1024
BATCH = 32
SEQ = 256
DEPTH = 2
DEC_BATCH = 8
DEC_SEQ = 4096
PAST_LEN = 512

GRID_W = 64
D_MIX = D_MODEL
HEAD_DIM = 64
D_A = D_MIX // 2
N_HEADS_A = D_A // HEAD_DIM
R_DECAY = 64
R_ICLR = 64
R_GATE = 128
D_B = D_MIX // 4
N_GROUPS_B = D_B // HEAD_DIM
CHUNK = 128
D_C = D_MIX - D_A - D_B
CONV_K = 31
N_EXPERTS = 16
N_EXPERT_GROUPS = 4
EXPERTS_PER_GROUP = N_EXPERTS // N_EXPERT_GROUPS
TOP_K = 2
D_EXPERT = D_MODEL // 4
MOE_BLOCK = 128
NORM_EPS = 1e-6
LN_EPS = 1e-5
LN_X_EPS = 64e-5
COLS_A = 3 * D_A + 2 * R_DECAY + 2 * R_ICLR + R_GATE
COLS_B = 2 * D_B
COLS_C = 2 * D_C
D_PROJ = COLS_A + COLS_B + COLS_C
SPLIT_A = [D_A, 2 * D_A, 3 * D_A, 3 * D_A + R_DECAY, 3 * D_A + 2 * R_DECAY,
           3 * D_A + 2 * R_DECAY + R_ICLR, 3 * D_A + 2 * R_DECAY + 2 * R_ICLR]

kernel_name = 'hybrid_rwkv7_gmlp_conformer_moe_dit_step'


def _rmsnorm(x, g):
    xf = x.astype(jnp.float32)
    y = xf * lax.rsqrt(jnp.mean(xf * xf, axis=-1, keepdims=True) + NORM_EPS)
    return (y * g.astype(jnp.float32)).astype(x.dtype)


def _layernorm(x, g, b):
    xf = x.astype(jnp.float32)
    mu = jnp.mean(xf, axis=-1, keepdims=True)
    var = jnp.mean(jnp.square(xf - mu), axis=-1, keepdims=True)
    y = (xf - mu) * lax.rsqrt(var + LN_EPS)
    return (y * g.astype(jnp.float32) + b.astype(jnp.float32)).astype(x.dtype)


def _head_norm(y, g, b):
    mu = jnp.mean(y, axis=-1, keepdims=True)
    var = jnp.mean(jnp.square(y - mu), axis=-1, keepdims=True)
    yn = (y - mu) * lax.rsqrt(var + LN_X_EPS)
    return yn.reshape(y.shape[0], y.shape[1], -1) * g.astype(jnp.float32) + b.astype(jnp.float32)


def _l2norm(t):
    tf = t.astype(jnp.float32)
    return tf / jnp.maximum(jnp.sqrt(jnp.sum(tf * tf, axis=-1, keepdims=True)), 1e-12)


def _centred_shift(p):
    prev = jnp.pad(p[:, :-1], ((0, 0), (1, 0), (0, 0)))
    nxt = jnp.pad(p[:, 1:], ((0, 0), (0, 1), (0, 0)))
    return 0.5 * (prev + nxt)


def _rwkv7_scan(s0, r, decay, kt, v, kk, a, reverse):
    xs = tuple(jnp.moveaxis(t.astype(jnp.float32), 1, 0) for t in (r, decay, kt, v, kk, a))

    def step(s, inp):
        r_t, w_t, k_t, v_t, kk_t, a_t = inp
        s_kk = jnp.einsum('bhvk,bhk->bhv', s, kk_t)
        s = (s * w_t[:, :, None, :]
             - s_kk[..., None] * (kk_t * a_t)[:, :, None, :]
             + v_t[..., None] * k_t[:, :, None, :])
        return s, jnp.einsum('bhvk,bhk->bhv', s, r_t)

    s, ys = lax.scan(step, s0.astype(jnp.float32), xs, reverse=reverse)
    return s, jnp.moveaxis(ys, 0, 1)


def _depthwise_conv(x, w, b):
    y = lax.conv_general_dilated(x, w[:, None, :].astype(x.dtype), (1,),
                                 [(CONV_K // 2, CONV_K // 2)],
                                 dimension_numbers=('NWC', 'WIO', 'NWC'),
                                 feature_group_count=x.shape[-1])
    return y + b


def _token_mixers(h, s0, grid_w, lp):
    B, T, _ = h.shape
    p = h @ lp['w_in']
    pa, pb, pc = jnp.split(p, [COLS_A, COLS_A + COLS_B], axis=-1)

    pa = pa + lp['mu_shift'] * (_centred_shift(pa) - pa)
    r, k, v, wd_f, wd_b, ad_f, ad_b, gd = jnp.split(pa, SPLIT_A, axis=-1)

    def heads(t):
        return t.reshape(B, T, N_HEADS_A, HEAD_DIM)

    g = jax.nn.sigmoid(gd) @ lp['g2']
    ys, kts, states = [], [], []
    for d, (wd, ad) in enumerate(((wd_f, ad_f), (wd_b, ad_b))):
        wlog = -jax.nn.softplus(-(lp['w0'][d] + jnp.tanh(wd) @ lp['w2'][d])) - 0.5
        decay = jnp.exp(-jnp.exp(wlog.astype(jnp.float32)))
        a = jax.nn.sigmoid(lp['a0'][d] + ad @ lp['a2'][d])
        kk = _l2norm(heads(k * lp['k_k'][d]))
        kt = k * (1 + (a - 1) * lp['k_a'][d])
        s_d, y_d = _rwkv7_scan(s0[:, d], heads(r), heads(decay), heads(kt), heads(v), kk,
                               heads(a), reverse=(d == 1))
        ys.append(y_d)
        kts.append(kt)
        states.append(s_d)
    y = _head_norm(ys[0] + ys[1], lp['lnx_g'], lp['lnx_b'])
    bonus = jnp.sum(heads(r) * heads(kts[0] + kts[1]) * lp['r_k'], axis=-1, keepdims=True) * heads(v)
    ya = ((y + bonus.reshape(B, T, D_A)) * g).astype(h.dtype)

    u, vg = jnp.split(jax.nn.gelu(pb), 2, axis=-1)
    vg = _layernorm(vg, lp['gmlp_norm_g'], lp['gmlp_norm_b'])
    vg = vg.reshape(B, T // CHUNK, CHUNK, N_GROUPS_B, HEAD_DIM)
    sv = jnp.einsum('gpq,bnqgc->bnpgc', lp['gmlp_ws'], vg) + lp['gmlp_bs'].T[:, :, None]
    yb = _rmsnorm(u * sv.reshape(B, T, D_B), lp['beta_b'])

    a_c, b_c = jnp.split(pc, 2, axis=-1)
    gl = a_c * jax.nn.sigmoid(b_c)
    if grid_w is not None:
        rows = T // grid_w
        gl = gl.reshape(B * rows, grid_w, D_C)
    yc = _depthwise_conv(gl, lp['conv_w'], lp['conv_b']).reshape(B, T, D_C)
    yc = _rmsnorm(jax.nn.silu(_layernorm(yc, lp['conv_norm_g'], lp['conv_norm_b'])), lp['beta_c'])

    out = jnp.concatenate([ya, yb, yc], axis=-1) @ lp['w_out']
    return out, jnp.stack(states, axis=1).astype(h.dtype)


def _moe(h, lp):
    B, T, D = h.shape
    ht = h.reshape(B * T, D)
    probs = jax.nn.softmax((ht @ lp['w_router']).astype(jnp.float32), axis=-1)
    sel = probs + lp['b_router'].astype(jnp.float32)
    grp_score = lax.top_k(sel.reshape(-1, N_EXPERT_GROUPS, EXPERTS_PER_GROUP), TOP_K)[0].sum(-1)
    best = jnp.argmax(grp_score, axis=-1)
    in_grp = (jnp.arange(N_EXPERTS) // EXPERTS_PER_GROUP)[None, :] == best[:, None]
    _, idx = lax.top_k(jnp.where(in_grp, sel, -jnp.inf), TOP_K)
    w = jnp.take_along_axis(probs, idx, axis=-1)
    w = w / jnp.sum(w, axis=-1, keepdims=True)
    gates = jnp.sum(jax.nn.one_hot(idx, N_EXPERTS, dtype=jnp.float32) * w[..., None], axis=1).astype(h.dtype)

    def block(args):
        hb, gb = args
        gu = jnp.einsum('td,edf->tef', hb, lp['moe_w_gu'])
        g_, u_ = jnp.split(gu, 2, axis=-1)
        act = jax.nn.silu(g_) * u_ * gb[..., None]
        return jnp.einsum('tef,efd->td', act, lp['moe_w_down'])

    out = lax.map(block, (ht.reshape(-1, MOE_BLOCK, D), gates.reshape(-1, MOE_BLOCK, N_EXPERTS)))
    return out.reshape(B, T, D)


def _layer(x, mod, s0, grid_w, lp):
    sh1, sc1, gt1, sh2, sc2, gt2 = jnp.split(mod, 6, axis=-1)
    h = _rmsnorm(x, lp['norm1_g']) * (1 + sc1) + sh1
    mix, s_new = _token_mixers(h, s0, grid_w, lp)
    x = x + gt1 * mix
    h = _rmsnorm(x, lp['norm2_g']) * (1 + sc2) + sh2
    x = x + gt2 * _moe(h, lp)
    return x, s_new


def setup_inputs(seed: int = 0) -> dict:
    key = jax.random.key(seed)
    ks = iter(jax.random.split(key, 64))

    def nrm(shape, scale=1.0):
        return scale * jax.random.normal(next(ks), shape, jnp.float32)

    def gain(shape):
        return 1.0 + nrm(shape, 0.02)

    L = DEPTH
    return {
        'x_prompt': nrm((BATCH, SEQ, D_MODEL)),
        'x_sample': nrm((DEC_BATCH, DEC_SEQ, D_MODEL)),
        'state_rwkv': nrm((DEC_BATCH, DEPTH, 2, N_HEADS_A, HEAD_DIM, HEAD_DIM), 0.5),
        'c': nrm((DEC_BATCH, D_MODEL)),
        'c_ctx': nrm((D_MODEL,)),
        'norm1_g': gain((L, D_MODEL)),
        'norm2_g': gain((L, D_MODEL)),
        'ada_w': nrm((L, D_MODEL, 6 * D_MODEL), 0.5 * D_MODEL ** -0.5),
        'ada_b': nrm((L, 6 * D_MODEL), 0.02),
        'w_in': nrm((L, D_MODEL, D_PROJ), D_MODEL ** -0.5),
        'mu_shift': jax.random.uniform(next(ks), (L, COLS_A), jnp.float32),
        'w0': nrm((L, 2, D_A), 1.5) - 2.0,
        'w2': nrm((L, 2, R_DECAY, D_A), R_DECAY ** -0.5),
        'a0': nrm((L, 2, D_A), 0.1),
        'a2': nrm((L, 2, R_ICLR, D_A), R_ICLR ** -0.5),
        'k_k': 0.85 + nrm((L, 2, D_A), 0.02),
        'k_a': gain((L, 2, D_A)),
        'g2': nrm((L, R_GATE, D_A), R_GATE ** -0.5),
        'r_k': nrm((L, N_HEADS_A, HEAD_DIM), 0.1),
        'lnx_g': gain((L, D_A)),
        'lnx_b': nrm((L, D_A), 0.02),
        'gmlp_norm_g': gain((L, D_B)),
        'gmlp_norm_b': nrm((L, D_B), 0.02),
        'gmlp_ws': nrm((L, N_GROUPS_B, CHUNK, CHUNK), CHUNK ** -0.5),
        'gmlp_bs': gain((L, N_GROUPS_B, CHUNK)),
        'beta_b': gain((L, D_B)),
        'conv_w': nrm((L, CONV_K, D_C), CONV_K ** -0.5),
        'conv_b': nrm((L, D_C), 0.02),
        'conv_norm_g': gain((L, D_C)),
        'conv_norm_b': nrm((L, D_C), 0.02),
        'beta_c': gain((L, D_C)),
        'w_out': nrm((L, D_MIX, D_MODEL), D_MIX ** -0.5),
        'w_router': nrm((D_MODEL, N_EXPERTS), D_MODEL ** -0.5),
        'b_router': nrm((N_EXPERTS,), 0.01),
        'moe_w_gu': nrm((L, N_EXPERTS, D_MODEL, 2 * D_EXPERT), D_MODEL ** -0.5),
        'moe_w_down': nrm((L, N_EXPERTS, D_EXPERT, D_MODEL), D_EXPERT ** -0.5),
        'final_g': gain((D_MODEL,)),
    }


def reference(x_prompt, x_sample, state_rwkv, c, c_ctx, norm1_g, norm2_g, ada_w, ada_b, w_in,
              mu_shift, w0, w2, a0, a2, k_k, k_a, g2, r_k, lnx_g, lnx_b, gmlp_norm_g, gmlp_norm_b,
              gmlp_ws, gmlp_bs, beta_b, conv_w, conv_b, conv_norm_g, conv_norm_b, beta_c, w_out,
              w_router, b_router, moe_w_gu, moe_w_down, final_g):
    def layer_params(l):
        return dict(norm1_g=norm1_g[l], norm2_g=norm2_g[l], w_in=w_in[l], mu_shift=mu_shift[l],
                    w0=w0[l], w2=w2[l], a0=a0[l], a2=a2[l], k_k=k_k[l], k_a=k_a[l], g2=g2[l],
                    r_k=r_k[l], lnx_g=lnx_g[l], lnx_b=lnx_b[l], gmlp_norm_g=gmlp_norm_g[l],
                    gmlp_norm_b=gmlp_norm_b[l], gmlp_ws=gmlp_ws[l], gmlp_bs=gmlp_bs[l],
                    beta_b=beta_b[l], conv_w=conv_w[l], conv_b=conv_b[l],
                    conv_norm_g=conv_norm_g[l], conv_norm_b=conv_norm_b[l], beta_c=beta_c[l],
                    w_out=w_out[l], w_router=w_router, b_router=b_router,
                    moe_w_gu=moe_w_gu[l], moe_w_down=moe_w_down[l])

    xp = x_prompt
    s_zero = jnp.zeros((x_prompt.shape[0], 2, N_HEADS_A, HEAD_DIM, HEAD_DIM), x_prompt.dtype)
    ctx_states = []
    for l in range(DEPTH):
        mod = (jax.nn.silu(c_ctx) @ ada_w[l] + ada_b[l])[None, None, :]
        xp, s_l = _layer(xp, mod, s_zero, None, layer_params(l))
        ctx_states.append(s_l)
    new_state_rwkv = jnp.stack(ctx_states, axis=1)

    xs = x_sample
    for l in range(DEPTH):
        mod = (jax.nn.silu(c) @ ada_w[l] + ada_b[l])[:, None, :]
        xs, _ = _layer(xs, mod, state_rwkv[:, l], GRID_W, layer_params(l))

    y_prompt = _rmsnorm(xp, final_g)
    y_sample = _rmsnorm(xs, final_g)
    return (y_prompt, y_sample, new_state_rwkv)
```

```python
import functools
import math

import jax
import jax.numpy as jnp
from jax import lax
from jax.experimental import pallas as pl
from jax.experimental.pallas import tpu as pltpu

D_MODEL = 1024
DEPTH = 2
GRID_W = 64
HEAD_DIM = 64
D_A = 512
N_HEADS_A = 8
R_DECAY = 64
R_ICLR = 64
R_GATE = 128
D_B = 256
N_GROUPS_B = 4
GMLP_CHUNK = 128
D_C = 256
CONV_K = 31
CONV_PAD = CONV_K // 2
N_EXPERTS = 16
N_EXPERT_GROUPS = 4
EXPERTS_PER_GROUP = 4
D_EXPERT = 256
NORM_EPS = 1e-6
LN_EPS = 1e-5
LN_X_EPS = 64e-5
COLS_A = 3 * D_A + 2 * R_DECAY + 2 * R_ICLR + R_GATE
COLS_B = 2 * D_B
COLS_C = 2 * D_C
D_PROJ = COLS_A + COLS_B + COLS_C
COL_WD = 3 * D_A
COL_AD = COL_WD + 2 * R_DECAY
COL_GD = COL_AD + 2 * R_ICLR

SUBLANES = 8
TIME_TILE = 256
SCAN_CHUNK = 64
CONV_HALO = 16
ROW_TILE = 512
MOE_ROW_TILE = 1024
MOD_COL_TILE = 1536
VMEM_LIMIT = 56 * 1024 * 1024

BF16 = jnp.bfloat16
F32 = jnp.float32

_NT = (((1,), (1,)), ((), ()))
_TN = (((0,), (0,)), ((), ()))


def _dot(a, b):
    return jnp.dot(a, b, preferred_element_type=F32)


def _dot_nt(a, b):
    return lax.dot_general(a, b, _NT, preferred_element_type=F32)


def _dot_tn(a, b):
    return lax.dot_general(a, b, _TN, preferred_element_type=F32)


def _split_bf16(x, parts):
    out = []
    for _ in range(parts):
        p = x.astype(BF16)
        out.append(p)
        x = x - p.astype(F32)
    return out


def _sigmoid(x):
    return 1.0 / (1.0 + jnp.exp(-x))


def _group_sum(x, bd):
    hi, lo = _split_bf16(x, 2)
    return _dot(hi, bd) + _dot(lo, bd)


def _full(shape):
    zeros = (0,) * len(shape)
    return pl.BlockSpec(shape, lambda *_: zeros)


def _params(*sem):
    return pltpu.CompilerParams(dimension_semantics=sem, vmem_limit_bytes=VMEM_LIMIT)


def _mod_kernel(c_ref, w_ref, b_ref, o_ref):
    c = c_ref[...]
    s = c * _sigmoid(c)
    o_ref[0] = _dot(s.astype(BF16), w_ref[0].astype(BF16)) + b_ref[0]


def _modulation(c_all, ada_w, ada_b):
    rows = c_all.shape[0]
    n_col = ada_w.shape[-1] // MOD_COL_TILE
    return pl.pallas_call(
        _mod_kernel,
        grid=(DEPTH, n_col),
        in_specs=[
            _full((rows, D_MODEL)),
            pl.BlockSpec((1, D_MODEL, MOD_COL_TILE), lambda l, j: (l, 0, j)),
            pl.BlockSpec((1, 1, MOD_COL_TILE), lambda l, j: (l, 0, j)),
        ],
        out_specs=pl.BlockSpec((1, rows, MOD_COL_TILE), lambda l, j: (l, 0, j)),
        out_shape=jax.ShapeDtypeStruct((DEPTH, rows, ada_w.shape[-1]), F32),
        compiler_params=_params("arbitrary", "arbitrary"),
        name="adaln_modulation",
    )(c_all, ada_w, ada_b.reshape(DEPTH, 1, -1))


def _modulated_rmsnorm(x, gain, shift, scale):
    y = x * lax.rsqrt(jnp.mean(x * x, axis=-1, keepdims=True) + NORM_EPS)
    return y * gain * (1.0 + scale) + shift


def _inproj_kernel(x_ref, mod_ref, g_ref, w_ref, pa_ref, pb_ref, pc_ref):
    mod = mod_ref[0]
    h = _modulated_rmsnorm(x_ref[...], g_ref[...], mod[:, 0:D_MODEL], mod[:, D_MODEL:2 * D_MODEL])
    p = _dot(h.astype(BF16), w_ref[...])
    pa_ref[...] = p[:, :COLS_A]
    pb_ref[...] = p[:, COLS_A:COLS_A + COLS_B]
    pc_ref[...] = p[:, COLS_A + COLS_B:]


def _mod_spec(seq_len, row_tile, per_batch):
    if per_batch:
        return pl.BlockSpec((1, 1, 6 * D_MODEL), lambda i, *_: (i * row_tile // seq_len, 0, 0))
    return pl.BlockSpec((1, 1, 6 * D_MODEL), lambda i, *_: (0, 0, 0))


def _in_projection(x, mod, norm_g, w_in, seq_len, per_batch):
    n = x.shape[0]
    row = lambda width: pl.BlockSpec((ROW_TILE, width), lambda i: (i, 0))
    return pl.pallas_call(
        _inproj_kernel,
        grid=(n // ROW_TILE,),
        in_specs=[row(D_MODEL), _mod_spec(seq_len, ROW_TILE, per_batch), _full((1, D_MODEL)),
                  _full((D_MODEL, D_PROJ))],
        out_specs=[row(COLS_A), row(COLS_B), row(COLS_C)],
        out_shape=[jax.ShapeDtypeStruct((n, COLS_A), F32), jax.ShapeDtypeStruct((n, COLS_B), F32),
                   jax.ShapeDtypeStruct((n, COLS_C), F32)],
        compiler_params=_params("arbitrary"),
        name="in_projection",
    )(x, mod, norm_g, w_in)


def _token_shift(pa_ref, prev_ref, next_ref, pad_ref, mu, is_first, is_last):
    tb = pa_ref.shape[0]
    cur = pa_ref[...]
    pad_ref[0:SUBLANES, :] = jnp.where(is_first, 0.0, prev_ref[...])
    pad_ref[SUBLANES:SUBLANES + tb, :] = cur
    pad_ref[SUBLANES + tb:2 * SUBLANES + tb, :] = jnp.where(is_last, 0.0, next_ref[...])
    prev = pad_ref[SUBLANES - 1:SUBLANES - 1 + tb, :]
    nxt = pad_ref[SUBLANES + 1:SUBLANES + 1 + tb, :]
    return cur + mu * (0.5 * (prev + nxt) - cur)


def _halo_specs(n_t, time_block):
    per_tile = TIME_TILE // SUBLANES

    def prev_map(*ids):
        b, tb = time_block(*ids)
        return (jnp.maximum((b * n_t + tb) * per_tile - 1, 0), 0)

    def next_map(*ids):
        b, tb = time_block(*ids)
        nxt = (b * n_t + tb + 1) * per_tile
        return (jnp.where(tb == n_t - 1, nxt - 1, nxt), 0)

    return (pl.BlockSpec((SUBLANES, COLS_A), prev_map), pl.BlockSpec((SUBLANES, COLS_A), next_map))


def _scan_kernel(pa_ref, prev_ref, next_ref, s0_ref, mu_ref, w0_ref, w2_ref, a0_ref, a2_ref, kk_ref, ka_ref,
                 bd_ref, y_ref, sout_ref,
                 pad_ref, r_s, lw_s, kt_s, v_s, kn_s, b_s, state_s, g_s, sa_s, q_s, y0_s, pc_s, *, n_t):
    d = pl.program_id(0)
    t = pl.program_id(2)
    tb = t + d * (n_t - 1 - 2 * t)
    n_chunks = TIME_TILE // SCAN_CHUNK
    c_len = SCAN_CHUNK

    @pl.when(t == 0)
    def _():
        state_s[...] = s0_ref[0, 0]

    ps = _token_shift(pa_ref, prev_ref, next_ref, pad_ref, mu_ref[...], tb == 0, tb == n_t - 1)
    r = ps[:, 0:D_A]
    k = ps[:, D_A:2 * D_A]
    v = ps[:, 2 * D_A:3 * D_A]
    wd = ps[:, COL_WD:COL_WD + 2 * R_DECAY]
    ad = ps[:, COL_AD:COL_AD + 2 * R_ICLR]
    wl = w0_ref[0] + _dot(jnp.tanh(wd).astype(BF16), w2_ref[0])
    lw = -math.exp(-0.5) * _sigmoid(wl)
    a = _sigmoid(a0_ref[0] + _dot(ad.astype(BF16), a2_ref[0]))
    kx = k * kk_ref[0]
    kn = kx * lax.rsqrt(jnp.maximum(_group_sum(kx * kx, bd_ref[...]), 1e-24))
    r_s[...] = r
    lw_s[...] = lw
    kt_s[...] = k * (1.0 + (a - 1.0) * ka_ref[0])
    v_s[...] = v
    kn_s[...] = kn
    b_s[...] = kn * a

    sign = 1 - 2 * d
    row = lax.broadcasted_iota(jnp.int32, (c_len, c_len), 0)
    col = lax.broadcasted_iota(jnp.int32, (c_len, c_len), 1)
    before = (row - col) * sign > 0
    upto = (row - col) * sign >= 0
    upto_bf = jnp.where(upto, 1.0, 0.0).astype(BF16)
    eye = jnp.where(row == col, 1.0, 0.0)

    def build(c, carry):
        rows = pl.ds(pl.multiple_of(c * c_len, c_len), c_len)
        lwc = lw_s[rows, :]
        cum = sum(_dot(upto_bf, piece) for piece in _split_bf16(lwc, 3))
        tot = jnp.sum(lwc, axis=0, keepdims=True)
        e_neg = jnp.exp(-cum)
        e_rest = jnp.exp(tot - cum)
        kap_all = kn_s[rows, :] * jnp.exp(cum - lwc)
        rh_all = r_s[rows, :] * jnp.exp(cum)
        kt_c = kt_s[rows, :]
        b_c = b_s[rows, :]
        kh_all = (kt_c * e_neg).astype(BF16)
        bh_all = (b_c * e_neg).astype(BF16)
        kq_all = (kt_c * e_rest).astype(BF16)
        bq_all = (b_c * e_rest).astype(BF16)
        v_all = v_s[rows, :].astype(BF16)
        pc_s[c] = jnp.exp(tot)
        for h in range(N_HEADS_A):
            sl = slice(h * HEAD_DIM, (h + 1) * HEAD_DIM)
            kap = kap_all[:, sl]
            rh = rh_all[:, sl]
            kap_bf = kap.astype(BF16)
            vh = v_all[:, sl]
            both = jnp.concatenate([kap_bf, rh.astype(BF16)], axis=0)
            xb = _dot_nt(both, bh_all[:, sl])
            xk = _dot_nt(both, kh_all[:, sl])
            l_b = jnp.where(before, xb[:c_len], 0.0)
            a_b = jnp.where(upto, xb[c_len:], 0.0).astype(BF16)
            l_k = jnp.where(before, xk[:c_len], 0.0).astype(BF16)
            a_k = jnp.where(upto, xk[c_len:], 0.0).astype(BF16)
            inv = eye - l_b
            lp = l_b.astype(BF16)
            lp = _dot(lp, lp)
            n_sq = int(math.log2(c_len)) - 1
            for j in range(n_sq):
                lp_bf = lp.astype(BF16)
                inv = inv + _dot(inv.astype(BF16), lp_bf)
                if j < n_sq - 1:
                    lp = _dot(lp_bf, lp_bf)
            inv_bf = inv.astype(BF16)
            lkv = _dot(l_k, vh)
            w = _dot(inv_bf, kap_bf).astype(BF16)
            u = _dot(inv_bf, lkv.astype(BF16)).astype(BF16)
            q_s[c, h] = rh - _dot(a_b, w)
            y0_s[c, h] = _dot(a_k, vh) - _dot(a_b, u)
            bq = bq_all[:, sl]
            g_s[c, h] = _dot_tn(w, bq)
            sa_s[c, h] = _dot_tn(vh, kq_all[:, sl]) - _dot_tn(u, bq)
        return carry

    lax.fori_loop(0, n_chunks, build, 0)

    def advance(i, carry):
        c = i + d * (n_chunks - 1 - 2 * i)
        rows = pl.ds(pl.multiple_of(c * c_len, c_len), c_len)
        pc = pc_s[c]
        ys = []
        for h in range(N_HEADS_A):
            s = state_s[h]
            s_bf = s.astype(BF16)
            ys.append(_dot_nt(q_s[c, h].astype(BF16), s_bf) + y0_s[c, h])
            state_s[h] = (s * pc[:, h * HEAD_DIM:(h + 1) * HEAD_DIM]
                          - _dot(s_bf, g_s[c, h].astype(BF16)) + sa_s[c, h])
        y_ref[0, rows, :] = jnp.concatenate(ys, axis=1)
        return carry

    lax.fori_loop(0, n_chunks, advance, 0)

    @pl.when(t == n_t - 1)
    def _():
        sout_ref[0, 0] = state_s[...]


def _rwkv_scan(pa, s0, wts, batch, seq_len):
    n = pa.shape[0]
    n_t = seq_len // TIME_TILE
    n_chunks = TIME_TILE // SCAN_CHUNK

    def time_block(d, b, t):
        return b, t + d * (n_t - 1 - 2 * t)

    def tile_map(d, b, t):
        return (b * n_t + t + d * (n_t - 1 - 2 * t), 0)

    prev_spec, next_spec = _halo_specs(n_t, time_block)
    per_dir = lambda shape: pl.BlockSpec((1,) + shape, lambda d, b, t: (d,) + (0,) * len(shape))
    head_mat = (n_chunks, N_HEADS_A, SCAN_CHUNK, HEAD_DIM)
    stream = pltpu.VMEM((TIME_TILE, D_A), F32)
    return pl.pallas_call(
        functools.partial(_scan_kernel, n_t=n_t),
        grid=(2, batch, n_t),
        in_specs=[
            pl.BlockSpec((TIME_TILE, COLS_A), tile_map), prev_spec, next_spec,
            pl.BlockSpec((1, 1, N_HEADS_A, HEAD_DIM, HEAD_DIM), lambda d, b, t: (b, d, 0, 0, 0)),
            _full((1, COLS_A)),
            per_dir((1, D_A)), per_dir((2 * R_DECAY, D_A)), per_dir((1, D_A)), per_dir((2 * R_ICLR, D_A)),
            per_dir((1, D_A)), per_dir((1, D_A)),
            _full((D_A, D_A)),
        ],
        out_specs=[
            pl.BlockSpec((1, TIME_TILE, D_A), lambda d, b, t: (d,) + tile_map(d, b, t)),
            pl.BlockSpec((1, 1, N_HEADS_A, HEAD_DIM, HEAD_DIM), lambda d, b, t: (b, d, 0, 0, 0)),
        ],
        out_shape=[jax.ShapeDtypeStruct((2, n, D_A), F32),
                   jax.ShapeDtypeStruct((batch, 2, N_HEADS_A, HEAD_DIM, HEAD_DIM), F32)],
        scratch_shapes=[
            pltpu.VMEM((TIME_TILE + 2 * SUBLANES, COLS_A), F32),
            stream, stream, stream, stream, stream, stream,
            pltpu.VMEM((N_HEADS_A, HEAD_DIM, HEAD_DIM), F32),
            pltpu.VMEM((n_chunks, N_HEADS_A, HEAD_DIM, HEAD_DIM), F32),
            pltpu.VMEM((n_chunks, N_HEADS_A, HEAD_DIM, HEAD_DIM), F32),
            pltpu.VMEM(head_mat, F32),
            pltpu.VMEM(head_mat, F32),
            pltpu.VMEM((n_chunks, 1, D_A), F32),
        ],
        compiler_params=_params("arbitrary", "arbitrary", "arbitrary"),
        name="rwkv7_scan",
    )(pa, pa, pa, s0, wts["mu_shift"], wts["w0"], wts["w2_pad"], wts["a0"], wts["a2_pad"], wts["k_k"],
      wts["k_a"], wts["head_blocks"])


def _layernorm(x, g, b):
    mu = jnp.mean(x, axis=-1, keepdims=True)
    xc = x - mu
    var = jnp.mean(xc * xc, axis=-1, keepdims=True)
    return xc * lax.rsqrt(var + LN_EPS) * g + b


def _rmsnorm(x, g):
    return x * lax.rsqrt(jnp.mean(x * x, axis=-1, keepdims=True) + NORM_EPS) * g


def _mixer_kernel(pa_ref, prev_ref, next_ref, pb_ref, pc_ref, yf_ref, yb_ref,
                  mu_ref, a0_ref, a2_ref, ka_ref, g2_ref, rk_ref, lnxg_ref, lnxb_ref, bd_ref,
                  gng_ref, gnb_ref, ws_ref, bs_ref, betab_ref,
                  cw_ref, cb_ref, cng_ref, cnb_ref, betac_ref,
                  cat_ref, pad_ref, cpad_ref, *, n_t, conv_len):
    t = pl.program_id(1)
    bd = bd_ref[...]

    ps = _token_shift(pa_ref, prev_ref, next_ref, pad_ref, mu_ref[...], t == 0, t == n_t - 1)
    r = ps[:, 0:D_A]
    k = ps[:, D_A:2 * D_A]
    v = ps[:, 2 * D_A:3 * D_A]
    ad = ps[:, COL_AD:COL_AD + 2 * R_ICLR].astype(BF16)
    gd = ps[:, COL_GD:COL_GD + R_GATE]
    a_f = _sigmoid(a0_ref[0] + _dot(ad, a2_ref[0]))
    a_b = _sigmoid(a0_ref[1] + _dot(ad, a2_ref[1]))
    kt_sum = k * (2.0 + (a_f - 1.0) * ka_ref[0] + (a_b - 1.0) * ka_ref[1])
    bonus = _group_sum(r * kt_sum * rk_ref[...], bd) * v
    gate = _dot(_sigmoid(gd).astype(BF16), g2_ref[...])
    y = yf_ref[0] + yb_ref[0]
    mean = _group_sum(y, bd) * (1.0 / HEAD_DIM)
    yc = y - mean
    var = _group_sum(yc * yc, bd) * (1.0 / HEAD_DIM)
    yn = yc * lax.rsqrt(var + LN_X_EPS) * lnxg_ref[...] + lnxb_ref[...]
    cat_ref[:, 0:D_A] = ((yn + bonus) * gate).astype(BF16)

    gb = jax.nn.gelu(pb_ref[...])
    u = gb[:, :D_B]
    vg = _layernorm(gb[:, D_B:], gng_ref[...], gnb_ref[...]).astype(BF16)
    head_of_lane = lax.broadcasted_iota(jnp.int32, (GMLP_CHUNK, D_B), 1) // HEAD_DIM
    sv_chunks = []
    for n in range(TIME_TILE // GMLP_CHUNK):
        vgc = vg[n * GMLP_CHUNK:(n + 1) * GMLP_CHUNK]
        sv = bs_ref[...]
        for g in range(N_GROUPS_B):
            sv = sv + jnp.where(head_of_lane == g, _dot(ws_ref[g], vgc), 0.0)
        sv_chunks.append(sv)
    sv = jnp.concatenate(sv_chunks, axis=0)
    cat_ref[:, D_A:D_A + D_B] = _rmsnorm(u * sv, betab_ref[...]).astype(BF16)

    pc = pc_ref[...]
    gl = pc[:, :D_C] * _sigmoid(pc[:, D_C:])
    stride = conv_len + 2 * CONV_HALO
    zeros = jnp.zeros((CONV_HALO, D_C), F32)
    convs = []
    for q in range(TIME_TILE // conv_len):
        base = q * stride
        cpad_ref[base:base + CONV_HALO, :] = zeros
        cpad_ref[base + CONV_HALO:base + CONV_HALO + conv_len, :] = gl[q * conv_len:(q + 1) * conv_len]
        cpad_ref[base + CONV_HALO + conv_len:base + stride, :] = zeros
    for q in range(TIME_TILE // conv_len):
        base = q * stride + CONV_HALO - CONV_PAD
        acc = jnp.zeros((conv_len, D_C), F32) + cb_ref[...]
        for j in range(CONV_K):
            acc = acc + cw_ref[j:j + 1, :] * cpad_ref[base + j:base + j + conv_len, :]
        convs.append(acc)
    conv = jnp.concatenate(convs, axis=0) if len(convs) > 1 else convs[0]
    z = _layernorm(conv, cng_ref[...], cnb_ref[...])
    z = z * _sigmoid(z)
    cat_ref[:, D_A + D_B:] = _rmsnorm(z, betac_ref[...]).astype(BF16)


def _token_mixers(pa, pb, pc, y_scan, wts, batch, seq_len, conv_len):
    n = pa.shape[0]
    n_t = seq_len // TIME_TILE

    def time_block(b, t):
        return b, t

    tile = lambda width: pl.BlockSpec((TIME_TILE, width), lambda b, t: (b * n_t + t, 0))
    prev_spec, next_spec = _halo_specs(n_t, time_block)
    y_dir = lambda d: pl.BlockSpec((1, TIME_TILE, D_A), lambda b, t: (d, b * n_t + t, 0))
    n_conv = TIME_TILE // conv_len
    return pl.pallas_call(
        functools.partial(_mixer_kernel, n_t=n_t, conv_len=conv_len),
        grid=(batch, n_t),
        in_specs=[
            tile(COLS_A), prev_spec, next_spec, tile(COLS_B), tile(COLS_C), y_dir(0), y_dir(1),
            _full((1, COLS_A)), _full((2, 1, D_A)), _full((2, 2 * R_ICLR, D_A)), _full((2, 1, D_A)),
            _full((R_GATE, D_A)), _full((1, D_A)), _full((1, D_A)), _full((1, D_A)), _full((D_A, D_A)),
            _full((1, D_B)), _full((1, D_B)), _full((N_GROUPS_B, GMLP_CHUNK, GMLP_CHUNK)),
            _full((GMLP_CHUNK, D_B)), _full((1, D_B)),
            _full((CONV_K, D_C)), _full((1, D_C)), _full((1, D_C)), _full((1, D_C)), _full((1, D_C)),
        ],
        out_specs=tile(D_MODEL),
        out_shape=jax.ShapeDtypeStruct((n, D_MODEL), BF16),
        scratch_shapes=[
            pltpu.VMEM((TIME_TILE + 2 * SUBLANES, COLS_A), F32),
            pltpu.VMEM((n_conv * (conv_len + 2 * CONV_HALO), D_C), F32),
        ],
        compiler_params=_params("arbitrary", "arbitrary"),
        name="token_mixers",
    )(pa, pa, pa, pb, pc, y_scan, y_scan,
      wts["mu_shift"], wts["a0"], wts["a2_pad"], wts["k_a"], wts["g2"], wts["r_k"], wts["lnx_g"], wts["lnx_b"],
      wts["head_blocks"], wts["gmlp_norm_g"], wts["gmlp_norm_b"], wts["gmlp_ws"], wts["gmlp_bias"],
      wts["beta_b"], wts["conv_w"], wts["conv_b"], wts["conv_norm_g"], wts["conv_norm_b"], wts["beta_c"])


def _top2_sum(a, b, c, d):
    hi1, lo1 = jnp.maximum(a, b), jnp.minimum(a, b)
    hi2, lo2 = jnp.maximum(c, d), jnp.minimum(c, d)
    return jnp.maximum(hi1, hi2) + jnp.maximum(jnp.minimum(hi1, hi2), jnp.maximum(lo1, lo2))


def _router_gates(logits_t, bias_ref):
    m = jnp.max(logits_t, axis=0, keepdims=True)
    e = jnp.exp(logits_t - m)
    probs = e / jnp.sum(e, axis=0, keepdims=True)
    sel = probs + bias_ref[...]
    p_row = [probs[i:i + 1, :] for i in range(N_EXPERTS)]
    s_row = [sel[i:i + 1, :] for i in range(N_EXPERTS)]
    best_val = None
    best = None
    for g in range(N_EXPERT_GROUPS):
        score = _top2_sum(*s_row[g * EXPERTS_PER_GROUP:(g + 1) * EXPERTS_PER_GROUP])
        if g == 0:
            best_val, best = score, jnp.zeros_like(score, dtype=jnp.int32)
        else:
            better = score > best_val
            best_val = jnp.where(better, score, best_val)
            best = jnp.where(better, g, best)
    chosen = []
    for i in range(N_EXPERTS):
        g = i // EXPERTS_PER_GROUP
        rank = jnp.zeros_like(best)
        for j in range(g * EXPERTS_PER_GROUP, (g + 1) * EXPERTS_PER_GROUP):
            if j == i:
                continue
            ahead = (s_row[j] >= s_row[i]) if j < i else (s_row[j] > s_row[i])
            rank = rank + ahead.astype(jnp.int32)
        chosen.append((best == g) & (rank < 2))
    picked = [jnp.where(chosen[i], p_row[i], 0.0) for i in range(N_EXPERTS)]
    denom = picked[0]
    for i in range(1, N_EXPERTS):
        denom = denom + picked[i]
    return jnp.concatenate([p / denom for p in picked], axis=0)


def _outproj_kernel(cat_ref, x_ref, mod_ref, wout_ref, g_ref, wr_hi_ref, wr_lo_ref, br_ref,
                    x1_ref, h2_ref, gates_ref):
    mod = mod_ref[0]
    out = _dot(cat_ref[...], wout_ref[...])
    x1 = x_ref[...] + mod[:, 2 * D_MODEL:3 * D_MODEL] * out
    x1_ref[...] = x1
    h2 = _modulated_rmsnorm(x1, g_ref[...], mod[:, 3 * D_MODEL:4 * D_MODEL], mod[:, 4 * D_MODEL:5 * D_MODEL])
    h_hi, h_lo = _split_bf16(h2, 2)
    h2_ref[...] = h_hi
    logits_t = (_dot_nt(wr_hi_ref[...], h_hi) + _dot_nt(wr_hi_ref[...], h_lo) + _dot_nt(wr_lo_ref[...], h_hi))
    gates_ref[...] = _router_gates(logits_t, br_ref)


def _out_projection(cat, x, mod, w_out, norm_g, wr_hi, wr_lo, b_router, seq_len, per_batch):
    n = x.shape[0]
    row = lambda width: pl.BlockSpec((ROW_TILE, width), lambda i: (i, 0))
    return pl.pallas_call(
        _outproj_kernel,
        grid=(n // ROW_TILE,),
        in_specs=[row(D_MODEL), row(D_MODEL), _mod_spec(seq_len, ROW_TILE, per_batch),
                  _full((D_MODEL, D_MODEL)), _full((1, D_MODEL)),
                  _full((N_EXPERTS, D_MODEL)), _full((N_EXPERTS, D_MODEL)), _full((N_EXPERTS, 1))],
        out_specs=[row(D_MODEL), row(D_MODEL), pl.BlockSpec((N_EXPERTS, ROW_TILE), lambda i: (0, i))],
        out_shape=[jax.ShapeDtypeStruct((n, D_MODEL), F32), jax.ShapeDtypeStruct((n, D_MODEL), BF16),
                   jax.ShapeDtypeStruct((N_EXPERTS, n), F32)],
        compiler_params=_params("arbitrary"),
        name="out_projection_router",
    )(cat, x, mod, w_out, norm_g, wr_hi, wr_lo, b_router)


def _moe_kernel(h_ref, gates_ref, x1_ref, mod_ref, wgu_ref, wdn_ref, fg_ref, o_ref, acc_ref, *, final_norm):
    e = pl.program_id(1)

    @pl.when(e == 0)
    def _():
        acc_ref[...] = jnp.zeros_like(acc_ref)

    gates = gates_ref[...]
    lane = lax.broadcasted_iota(jnp.int32, gates.shape, 1)
    gate = jnp.sum(jnp.where(lane == e, gates, 0.0), axis=1, keepdims=True)
    gu = _dot(h_ref[...], wgu_ref[0])
    g = gu[:, :D_EXPERT]
    act = g * _sigmoid(g) * gu[:, D_EXPERT:] * gate
    acc_ref[...] += _dot(act.astype(BF16), wdn_ref[0])

    @pl.when(e == N_EXPERTS - 1)
    def _():
        x2 = x1_ref[...] + mod_ref[0][:, 5 * D_MODEL:6 * D_MODEL] * acc_ref[...]
        if final_norm:
            x2 = _rmsnorm(x2, fg_ref[...])
        o_ref[...] = x2


def _mixture_of_experts(h2, gates, x1, mod, w_gu, w_down, final_g, seq_len, per_batch, final_norm):
    n = x1.shape[0]
    row = lambda width: pl.BlockSpec((MOE_ROW_TILE, width), lambda i, e: (i, 0))
    return pl.pallas_call(
        functools.partial(_moe_kernel, final_norm=final_norm),
        grid=(n // MOE_ROW_TILE, N_EXPERTS),
        in_specs=[row(D_MODEL), row(N_EXPERTS), row(D_MODEL), _mod_spec(seq_len, MOE_ROW_TILE, per_batch),
                  pl.BlockSpec((1, D_MODEL, 2 * D_EXPERT), lambda i, e: (e, 0, 0)),
                  pl.BlockSpec((1, D_EXPERT, D_MODEL), lambda i, e: (e, 0, 0)),
                  _full((1, D_MODEL))],
        out_specs=row(D_MODEL),
        out_shape=jax.ShapeDtypeStruct((n, D_MODEL), F32),
        scratch_shapes=[pltpu.VMEM((MOE_ROW_TILE, D_MODEL), F32)],
        compiler_params=_params("arbitrary", "arbitrary"),
        name="mixture_of_experts",
    )(h2, gates, x1, mod, w_gu, w_down, final_g)


def _pad_low_rank(w, rank):
    z = jnp.zeros_like(w[0])
    return jnp.stack([jnp.concatenate([w[0], z], axis=0), jnp.concatenate([z, w[1]], axis=0)]).astype(BF16)


def kernel(x_prompt, x_sample, state_rwkv, c, c_ctx, norm1_g, norm2_g, ada_w, ada_b, w_in, mu_shift, w0, w2, a0, a2, k_k, k_a, g2, r_k, lnx_g, lnx_b, gmlp_norm_g, gmlp_norm_b, gmlp_ws, gmlp_bs, beta_b, conv_w, conv_b, conv_norm_g, conv_norm_b, beta_c, w_out, w_router, b_router, moe_w_gu, moe_w_down, final_g):
    batch_p, seq_p, _ = x_prompt.shape
    batch_s, seq_s, _ = x_sample.shape

    n_cond = 1 + batch_s
    cond_rows = -(-n_cond // SUBLANES) * SUBLANES
    c_all = jnp.concatenate([c_ctx[None, :], c, jnp.zeros((cond_rows - n_cond, D_MODEL), F32)], axis=0)
    mod_all = _modulation(c_all, ada_w, ada_b)

    head_id = jnp.arange(D_A) // HEAD_DIM
    head_blocks = (head_id[:, None] == head_id[None, :]).astype(BF16)
    wr_t = w_router.T
    wr_hi = wr_t.astype(BF16)
    wr_lo = (wr_t - wr_hi.astype(F32)).astype(BF16)
    b_r = b_router.reshape(N_EXPERTS, 1)
    final_g2 = final_g.reshape(1, D_MODEL)

    groups = [
        dict(x=x_prompt.reshape(batch_p * seq_p, D_MODEL), batch=batch_p, seq=seq_p, conv_len=seq_p,
             per_batch=False, s0=None, mod_rows=slice(0, 1)),
        dict(x=x_sample.reshape(batch_s * seq_s, D_MODEL), batch=batch_s, seq=seq_s, conv_len=GRID_W,
             per_batch=True, s0=state_rwkv, mod_rows=slice(1, 1 + batch_s)),
    ]
    ctx_states = []
    outputs = []
    for grp in groups:
        x = grp["x"]
        batch, seq, per_batch = grp["batch"], grp["seq"], grp["per_batch"]
        for l in range(DEPTH):
            mod = mod_all[l, grp["mod_rows"]][:, None, :]
            wts = dict(
                mu_shift=mu_shift[l].reshape(1, COLS_A),
                w0=w0[l].reshape(2, 1, D_A), w2_pad=_pad_low_rank(w2[l], R_DECAY),
                a0=a0[l].reshape(2, 1, D_A), a2_pad=_pad_low_rank(a2[l], R_ICLR),
                k_k=k_k[l].reshape(2, 1, D_A), k_a=k_a[l].reshape(2, 1, D_A),
                g2=g2[l].astype(BF16), r_k=r_k[l].reshape(1, D_A),
                lnx_g=lnx_g[l].reshape(1, D_A), lnx_b=lnx_b[l].reshape(1, D_A), head_blocks=head_blocks,
                gmlp_norm_g=gmlp_norm_g[l].reshape(1, D_B), gmlp_norm_b=gmlp_norm_b[l].reshape(1, D_B),
                gmlp_ws=gmlp_ws[l].astype(BF16), gmlp_bias=jnp.repeat(gmlp_bs[l].T, HEAD_DIM, axis=1),
                beta_b=beta_b[l].reshape(1, D_B),
                conv_w=conv_w[l], conv_b=conv_b[l].reshape(1, D_C),
                conv_norm_g=conv_norm_g[l].reshape(1, D_C), conv_norm_b=conv_norm_b[l].reshape(1, D_C),
                beta_c=beta_c[l].reshape(1, D_C),
            )
            pa, pb, pc = _in_projection(x, mod, norm1_g[l].reshape(1, D_MODEL), w_in[l].astype(BF16), seq, per_batch)
            if grp["s0"] is None:
                s0 = jnp.zeros((batch, 2, N_HEADS_A, HEAD_DIM, HEAD_DIM), F32)
            else:
                s0 = grp["s0"][:, l]
            y_scan, s_new = _rwkv_scan(pa, s0, wts, batch, seq)
            cat = _token_mixers(pa, pb, pc, y_scan, wts, batch, seq, grp["conv_len"])
            x1, h2, gates_t = _out_projection(cat, x, mod, w_out[l].astype(BF16), norm2_g[l].reshape(1, D_MODEL),
                                              wr_hi, wr_lo, b_r, seq, per_batch)
            x = _mixture_of_experts(h2, gates_t.T, x1, mod, moe_w_gu[l].astype(BF16), moe_w_down[l].astype(BF16),
                                    final_g2, seq, per_batch, final_norm=(l == DEPTH - 1))
            if grp["s0"] is None:
                ctx_states.append(s_new)
        outputs.append(x.reshape(batch, seq, D_MODEL))
    return (outputs[0], outputs[1], jnp.stack(ctx_states, axis=1))
```

```python
import functools
import math

import jax
import jax.numpy as jnp
from jax import lax
from jax.experimental import pallas as pl
from jax.experimental.pallas import tpu as pltpu

D_MODEL = 1024
DEPTH = 2
GRID_W = 64
HEAD_DIM = 64
D_A = 512
N_HEADS_A = 8
R_DECAY = 64
R_ICLR = 64
R_GATE = 128
D_B = 256
N_GROUPS_B = 4
GMLP_CHUNK = 128
D_C = 256
CONV_K = 31
CONV_PAD = CONV_K // 2
N_EXPERTS = 16
N_EXPERT_GROUPS = 4
EXPERTS_PER_GROUP = 4
D_EXPERT = 256
NORM_EPS = 1e-6
LN_EPS = 1e-5
LN_X_EPS = 64e-5
COLS_A = 3 * D_A + 2 * R_DECAY + 2 * R_ICLR + R_GATE
COLS_B = 2 * D_B
COLS_C = 2 * D_C
D_PROJ = COLS_A + COLS_B + COLS_C
COL_WD = 3 * D_A
COL_AD = COL_WD + 2 * R_DECAY
COL_GD = COL_AD + 2 * R_ICLR

SUBLANES = 8
TIME_TILE = 256
SCAN_CHUNK = 64
BUILD_UNROLL = 4
CONV_HALO = 16
ROW_TILE = 512
MOE_ROW_TILE = 1024
MOD_COL_TILE = 1536
VMEM_LIMIT = 56 * 1024 * 1024

BF16 = jnp.bfloat16
F32 = jnp.float32

_NT = (((1,), (1,)), ((), ()))
_TN = (((0,), (0,)), ((), ()))


def _dot(a, b):
    return jnp.dot(a, b, preferred_element_type=F32)


def _dot_nt(a, b):
    return lax.dot_general(a, b, _NT, preferred_element_type=F32)


def _dot_tn(a, b):
    return lax.dot_general(a, b, _TN, preferred_element_type=F32)


def _split_bf16(x, parts):
    out = []
    for _ in range(parts):
        p = x.astype(BF16)
        out.append(p)
        x = x - p.astype(F32)
    return out


def _sigmoid(x):
    return 1.0 / (1.0 + jnp.exp(-x))


def _group_sum(x, bd):
    hi, lo = _split_bf16(x, 2)
    return _dot(hi, bd) + _dot(lo, bd)


def _full(shape):
    zeros = (0,) * len(shape)
    return pl.BlockSpec(shape, lambda *_: zeros)


def _params(*sem):
    return pltpu.CompilerParams(dimension_semantics=sem, vmem_limit_bytes=VMEM_LIMIT)


def _mod_kernel(c_ref, w_ref, b_ref, o_ref):
    c = c_ref[...]
    s = c * _sigmoid(c)
    o_ref[0] = _dot(s.astype(BF16), w_ref[0].astype(BF16)) + b_ref[0]


def _modulation(c_all, ada_w, ada_b):
    rows = c_all.shape[0]
    n_col = ada_w.shape[-1] // MOD_COL_TILE
    return pl.pallas_call(
        _mod_kernel,
        grid=(DEPTH, n_col),
        in_specs=[
            _full((rows, D_MODEL)),
            pl.BlockSpec((1, D_MODEL, MOD_COL_TILE), lambda l, j: (l, 0, j)),
            pl.BlockSpec((1, 1, MOD_COL_TILE), lambda l, j: (l, 0, j)),
        ],
        out_specs=pl.BlockSpec((1, rows, MOD_COL_TILE), lambda l, j: (l, 0, j)),
        out_shape=jax.ShapeDtypeStruct((DEPTH, rows, ada_w.shape[-1]), F32),
        compiler_params=_params("arbitrary", "arbitrary"),
        name="adaln_modulation",
    )(c_all, ada_w, ada_b.reshape(DEPTH, 1, -1))


def _modulated_rmsnorm(x, gain, shift, scale):
    y = x * lax.rsqrt(jnp.mean(x * x, axis=-1, keepdims=True) + NORM_EPS)
    return y * gain * (1.0 + scale) + shift


def _inproj_kernel(x_ref, mod_ref, g_ref, w_ref, pa_ref, pb_ref, pc_ref):
    mod = mod_ref[0]
    h = _modulated_rmsnorm(x_ref[...], g_ref[...], mod[:, 0:D_MODEL], mod[:, D_MODEL:2 * D_MODEL])
    p = _dot(h.astype(BF16), w_ref[...])
    pa_ref[...] = p[:, :COLS_A]
    pb_ref[...] = p[:, COLS_A:COLS_A + COLS_B]
    pc_ref[...] = p[:, COLS_A + COLS_B:]


def _mod_spec(seq_len, row_tile, per_batch):
    if per_batch:
        return pl.BlockSpec((1, 1, 6 * D_MODEL), lambda i, *_: (i * row_tile // seq_len, 0, 0))
    return pl.BlockSpec((1, 1, 6 * D_MODEL), lambda i, *_: (0, 0, 0))


def _in_projection(x, mod, norm_g, w_in, seq_len, per_batch):
    n = x.shape[0]
    row = lambda width: pl.BlockSpec((ROW_TILE, width), lambda i: (i, 0))
    return pl.pallas_call(
        _inproj_kernel,
        grid=(n // ROW_TILE,),
        in_specs=[row(D_MODEL), _mod_spec(seq_len, ROW_TILE, per_batch), _full((1, D_MODEL)),
                  _full((D_MODEL, D_PROJ))],
        out_specs=[row(COLS_A), row(COLS_B), row(COLS_C)],
        out_shape=[jax.ShapeDtypeStruct((n, COLS_A), F32), jax.ShapeDtypeStruct((n, COLS_B), F32),
                   jax.ShapeDtypeStruct((n, COLS_C), F32)],
        compiler_params=_params("arbitrary"),
        name="in_projection",
    )(x, mod, norm_g, w_in)


def _token_shift(pa_ref, prev_ref, next_ref, pad_ref, mu, is_first, is_last):
    tb = pa_ref.shape[0]
    cur = pa_ref[...]
    pad_ref[0:SUBLANES, :] = jnp.where(is_first, 0.0, prev_ref[...])
    pad_ref[SUBLANES:SUBLANES + tb, :] = cur
    pad_ref[SUBLANES + tb:2 * SUBLANES + tb, :] = jnp.where(is_last, 0.0, next_ref[...])
    prev = pad_ref[SUBLANES - 1:SUBLANES - 1 + tb, :]
    nxt = pad_ref[SUBLANES + 1:SUBLANES + 1 + tb, :]
    return cur + mu * (0.5 * (prev + nxt) - cur)


def _halo_specs(n_t, time_block):
    per_tile = TIME_TILE // SUBLANES

    def prev_map(*ids):
        b, tb = time_block(*ids)
        return (jnp.maximum((b * n_t + tb) * per_tile - 1, 0), 0)

    def next_map(*ids):
        b, tb = time_block(*ids)
        nxt = (b * n_t + tb + 1) * per_tile
        return (jnp.where(tb == n_t - 1, nxt - 1, nxt), 0)

    return (pl.BlockSpec((SUBLANES, COLS_A), prev_map), pl.BlockSpec((SUBLANES, COLS_A), next_map))


def _scan_kernel(pa_ref, prev_ref, next_ref, s0_ref, mu_ref, w0_ref, w2_ref, a0_ref, a2_ref, kk_ref, ka_ref,
                 bd_ref, y_ref, sout_ref,
                 pad_ref, r_s, lw_s, kt_s, v_s, kn_s, b_s, state_s, g_s, sa_s, q_s, y0_s, pc_s, *, n_t):
    d = pl.program_id(0)
    t = pl.program_id(2)
    tb = t + d * (n_t - 1 - 2 * t)
    n_chunks = TIME_TILE // SCAN_CHUNK
    c_len = SCAN_CHUNK

    @pl.when(t == 0)
    def _():
        state_s[...] = s0_ref[0, 0]

    ps = _token_shift(pa_ref, prev_ref, next_ref, pad_ref, mu_ref[...], tb == 0, tb == n_t - 1)
    r = ps[:, 0:D_A]
    k = ps[:, D_A:2 * D_A]
    v = ps[:, 2 * D_A:3 * D_A]
    wd = ps[:, COL_WD:COL_WD + 2 * R_DECAY]
    ad = ps[:, COL_AD:COL_AD + 2 * R_ICLR]
    wl = w0_ref[0] + _dot(jnp.tanh(wd).astype(BF16), w2_ref[0])
    lw = -math.exp(-0.5) * _sigmoid(wl)
    a = _sigmoid(a0_ref[0] + _dot(ad.astype(BF16), a2_ref[0]))
    kx = k * kk_ref[0]
    kn = kx * lax.rsqrt(jnp.maximum(_group_sum(kx * kx, bd_ref[...]), 1e-24))
    r_s[...] = r
    lw_s[...] = lw
    kt_s[...] = k * (1.0 + (a - 1.0) * ka_ref[0])
    v_s[...] = v
    kn_s[...] = kn
    b_s[...] = kn * a

    sign = 1 - 2 * d
    row = lax.broadcasted_iota(jnp.int32, (c_len, c_len), 0)
    col = lax.broadcasted_iota(jnp.int32, (c_len, c_len), 1)
    before = (row - col) * sign > 0
    upto = (row - col) * sign >= 0
    upto_bf = jnp.where(upto, 1.0, 0.0).astype(BF16)
    eye = jnp.where(row == col, 1.0, 0.0)

    def build(i, carry):
        head = lambda x, h: x[:, h * HEAD_DIM:(h + 1) * HEAD_DIM]
        chunk_ids = [i * BUILD_UNROLL + j for j in range(BUILD_UNROLL)]
        rh, kap_bf, vh, bh, kh, bq, kq = [], [], [], [], [], [], []
        for c in chunk_ids:
            rows = pl.ds(pl.multiple_of(c * c_len, c_len), c_len)
            lwc = lw_s[rows, :]
            cum = sum(_dot(upto_bf, piece) for piece in _split_bf16(lwc, 3))
            tot = jnp.sum(lwc, axis=0, keepdims=True)
            e_neg = jnp.exp(-cum)
            e_rest = jnp.exp(tot - cum)
            kap_all = (kn_s[rows, :] * jnp.exp(cum - lwc)).astype(BF16)
            rh_all = r_s[rows, :] * jnp.exp(cum)
            kt_c = kt_s[rows, :]
            b_c = b_s[rows, :]
            kh_all = (kt_c * e_neg).astype(BF16)
            bh_all = (b_c * e_neg).astype(BF16)
            kq_all = (kt_c * e_rest).astype(BF16)
            bq_all = (b_c * e_rest).astype(BF16)
            v_all = v_s[rows, :].astype(BF16)
            pc_s[c] = jnp.exp(tot)
            for h in range(N_HEADS_A):
                rh.append(head(rh_all, h))
                kap_bf.append(head(kap_all, h))
                vh.append(head(v_all, h))
                bh.append(head(bh_all, h))
                kh.append(head(kh_all, h))
                bq.append(head(bq_all, h))
                kq.append(head(kq_all, h))
        units = range(len(rh))
        both = [jnp.concatenate([kap_bf[n], rh[n].astype(BF16)], axis=0) for n in units]
        xb = [_dot_nt(both[n], bh[n]) for n in units]
        xk = [_dot_nt(both[n], kh[n]) for n in units]
        l_b = [jnp.where(before, xb[n][:c_len], 0.0) for n in units]
        a_b = [jnp.where(upto, xb[n][c_len:], 0.0).astype(BF16) for n in units]
        l_k = [jnp.where(before, xk[n][:c_len], 0.0).astype(BF16) for n in units]
        a_k = [jnp.where(upto, xk[n][c_len:], 0.0).astype(BF16) for n in units]
        lkv = [_dot(l_k[n], vh[n]).astype(BF16) for n in units]
        akv = [_dot(a_k[n], vh[n]) for n in units]
        inv = [eye - l_b[n] for n in units]
        lp_bf = [l_b[n].astype(BF16) for n in units]
        lp_bf = [_dot(lp_bf[n], lp_bf[n]).astype(BF16) for n in units]
        n_sq = int(math.log2(c_len)) - 1
        for j in range(n_sq):
            inv = [inv[n] + _dot(inv[n].astype(BF16), lp_bf[n]) for n in units]
            if j < n_sq - 1:
                lp_bf = [_dot(lp_bf[n], lp_bf[n]).astype(BF16) for n in units]
        inv_bf = [inv[n].astype(BF16) for n in units]
        w = [_dot(inv_bf[n], kap_bf[n]).astype(BF16) for n in units]
        u = [_dot(inv_bf[n], lkv[n]).astype(BF16) for n in units]
        abw = [_dot(a_b[n], w[n]) for n in units]
        abu = [_dot(a_b[n], u[n]) for n in units]
        g = [_dot_tn(w[n], bq[n]) for n in units]
        vkq = [_dot_tn(vh[n], kq[n]) for n in units]
        ubq = [_dot_tn(u[n], bq[n]) for n in units]
        for n in units:
            c, h = chunk_ids[n // N_HEADS_A], n % N_HEADS_A
            q_s[c, h] = rh[n] - abw[n]
            y0_s[c, h] = akv[n] - abu[n]
            g_s[c, h] = g[n]
            sa_s[c, h] = vkq[n] - ubq[n]
        return carry

    lax.fori_loop(0, n_chunks // BUILD_UNROLL, build, 0)

    def advance(i, carry):
        c = i + d * (n_chunks - 1 - 2 * i)
        rows = pl.ds(pl.multiple_of(c * c_len, c_len), c_len)
        pc = pc_s[c]
        heads = range(N_HEADS_A)
        s = [state_s[h] for h in heads]
        s_bf = [s[h].astype(BF16) for h in heads]
        sg = [_dot(s_bf[h], g_s[c, h].astype(BF16)) for h in heads]
        qs = [_dot_nt(q_s[c, h].astype(BF16), s_bf[h]) for h in heads]
        for h in heads:
            state_s[h] = s[h] * pc[:, h * HEAD_DIM:(h + 1) * HEAD_DIM] - sg[h] + sa_s[c, h]
        y_ref[0, rows, :] = jnp.concatenate([qs[h] + y0_s[c, h] for h in heads], axis=1)
        return carry

    lax.fori_loop(0, n_chunks, advance, 0, unroll=True)

    @pl.when(t == n_t - 1)
    def _():
        sout_ref[0, 0] = state_s[...]


def _rwkv_scan(pa, s0, wts, batch, seq_len):
    n = pa.shape[0]
    n_t = seq_len // TIME_TILE
    n_chunks = TIME_TILE // SCAN_CHUNK

    def time_block(d, b, t):
        return b, t + d * (n_t - 1 - 2 * t)

    def tile_map(d, b, t):
        return (b * n_t + t + d * (n_t - 1 - 2 * t), 0)

    prev_spec, next_spec = _halo_specs(n_t, time_block)
    per_dir = lambda shape: pl.BlockSpec((1,) + shape, lambda d, b, t: (d,) + (0,) * len(shape))
    head_mat = (n_chunks, N_HEADS_A, SCAN_CHUNK, HEAD_DIM)
    stream = pltpu.VMEM((TIME_TILE, D_A), F32)
    return pl.pallas_call(
        functools.partial(_scan_kernel, n_t=n_t),
        grid=(2, batch, n_t),
        in_specs=[
            pl.BlockSpec((TIME_TILE, COLS_A), tile_map), prev_spec, next_spec,
            pl.BlockSpec((1, 1, N_HEADS_A, HEAD_DIM, HEAD_DIM), lambda d, b, t: (b, d, 0, 0, 0)),
            _full((1, COLS_A)),
            per_dir((1, D_A)), per_dir((2 * R_DECAY, D_A)), per_dir((1, D_A)), per_dir((2 * R_ICLR, D_A)),
            per_dir((1, D_A)), per_dir((1, D_A)),
            _full((D_A, D_A)),
        ],
        out_specs=[
            pl.BlockSpec((1, TIME_TILE, D_A), lambda d, b, t: (d,) + tile_map(d, b, t)),
            pl.BlockSpec((1, 1, N_HEADS_A, HEAD_DIM, HEAD_DIM), lambda d, b, t: (b, d, 0, 0, 0)),
        ],
        out_shape=[jax.ShapeDtypeStruct((2, n, D_A), F32),
                   jax.ShapeDtypeStruct((batch, 2, N_HEADS_A, HEAD_DIM, HEAD_DIM), F32)],
        scratch_shapes=[
            pltpu.VMEM((TIME_TILE + 2 * SUBLANES, COLS_A), F32),
            stream, stream, stream, stream, stream, stream,
            pltpu.VMEM((N_HEADS_A, HEAD_DIM, HEAD_DIM), F32),
            pltpu.VMEM((n_chunks, N_HEADS_A, HEAD_DIM, HEAD_DIM), F32),
            pltpu.VMEM((n_chunks, N_HEADS_A, HEAD_DIM, HEAD_DIM), F32),
            pltpu.VMEM(head_mat, F32),
            pltpu.VMEM(head_mat, F32),
            pltpu.VMEM((n_chunks, 1, D_A), F32),
        ],
        compiler_params=_params("arbitrary", "arbitrary", "arbitrary"),
        name="rwkv7_scan",
    )(pa, pa, pa, s0, wts["mu_shift"], wts["w0"], wts["w2_pad"], wts["a0"], wts["a2_pad"], wts["k_k"],
      wts["k_a"], wts["head_blocks"])


def _layernorm(x, g, b):
    mu = jnp.mean(x, axis=-1, keepdims=True)
    xc = x - mu
    var = jnp.mean(xc * xc, axis=-1, keepdims=True)
    return xc * lax.rsqrt(var + LN_EPS) * g + b


def _rmsnorm(x, g):
    return x * lax.rsqrt(jnp.mean(x * x, axis=-1, keepdims=True) + NORM_EPS) * g


def _mixer_kernel(pa_ref, prev_ref, next_ref, pb_ref, pc_ref, yf_ref, yb_ref,
                  mu_ref, a0_ref, a2_ref, ka_ref, g2_ref, rk_ref, lnxg_ref, lnxb_ref, bd_ref,
                  gng_ref, gnb_ref, ws_ref, bs_ref, betab_ref,
                  cw_ref, cb_ref, cng_ref, cnb_ref, betac_ref,
                  cat_ref, pad_ref, cpad_ref, *, n_t, conv_len):
    t = pl.program_id(1)
    bd = bd_ref[...]

    ps = _token_shift(pa_ref, prev_ref, next_ref, pad_ref, mu_ref[...], t == 0, t == n_t - 1)
    r = ps[:, 0:D_A]
    k = ps[:, D_A:2 * D_A]
    v = ps[:, 2 * D_A:3 * D_A]
    ad = ps[:, COL_AD:COL_AD + 2 * R_ICLR].astype(BF16)
    gd = ps[:, COL_GD:COL_GD + R_GATE]
    a_f = _sigmoid(a0_ref[0] + _dot(ad, a2_ref[0]))
    a_b = _sigmoid(a0_ref[1] + _dot(ad, a2_ref[1]))
    kt_sum = k * (2.0 + (a_f - 1.0) * ka_ref[0] + (a_b - 1.0) * ka_ref[1])
    bonus = _group_sum(r * kt_sum * rk_ref[...], bd) * v
    gate = _dot(_sigmoid(gd).astype(BF16), g2_ref[...])
    y = yf_ref[0] + yb_ref[0]
    mean = _group_sum(y, bd) * (1.0 / HEAD_DIM)
    yc = y - mean
    var = _group_sum(yc * yc, bd) * (1.0 / HEAD_DIM)
    yn = yc * lax.rsqrt(var + LN_X_EPS) * lnxg_ref[...] + lnxb_ref[...]
    cat_ref[:, 0:D_A] = ((yn + bonus) * gate).astype(BF16)

    gb = jax.nn.gelu(pb_ref[...])
    u = gb[:, :D_B]
    vg = _layernorm(gb[:, D_B:], gng_ref[...], gnb_ref[...]).astype(BF16)
    head_of_lane = lax.broadcasted_iota(jnp.int32, (GMLP_CHUNK, D_B), 1) // HEAD_DIM
    sv_chunks = []
    for n in range(TIME_TILE // GMLP_CHUNK):
        vgc = vg[n * GMLP_CHUNK:(n + 1) * GMLP_CHUNK]
        sv = bs_ref[...]
        for g in range(N_GROUPS_B):
            sv = sv + jnp.where(head_of_lane == g, _dot(ws_ref[g], vgc), 0.0)
        sv_chunks.append(sv)
    sv = jnp.concatenate(sv_chunks, axis=0)
    cat_ref[:, D_A:D_A + D_B] = _rmsnorm(u * sv, betab_ref[...]).astype(BF16)

    pc = pc_ref[...]
    gl = pc[:, :D_C] * _sigmoid(pc[:, D_C:])
    stride = conv_len + 2 * CONV_HALO
    zeros = jnp.zeros((CONV_HALO, D_C), F32)
    convs = []
    for q in range(TIME_TILE // conv_len):
        base = q * stride
        cpad_ref[base:base + CONV_HALO, :] = zeros
        cpad_ref[base + CONV_HALO:base + CONV_HALO + conv_len, :] = gl[q * conv_len:(q + 1) * conv_len]
        cpad_ref[base + CONV_HALO + conv_len:base + stride, :] = zeros
    for q in range(TIME_TILE // conv_len):
        base = q * stride + CONV_HALO - CONV_PAD
        acc = jnp.zeros((conv_len, D_C), F32) + cb_ref[...]
        for j in range(CONV_K):
            acc = acc + cw_ref[j:j + 1, :] * cpad_ref[base + j:base + j + conv_len, :]
        convs.append(acc)
    conv = jnp.concatenate(convs, axis=0) if len(convs) > 1 else convs[0]
    z = _layernorm(conv, cng_ref[...], cnb_ref[...])
    z = z * _sigmoid(z)
    cat_ref[:, D_A + D_B:] = _rmsnorm(z, betac_ref[...]).astype(BF16)


def _token_mixers(pa, pb, pc, y_scan, wts, batch, seq_len, conv_len):
    n = pa.shape[0]
    n_t = seq_len // TIME_TILE

    def time_block(b, t):
        return b, t

    tile = lambda width: pl.BlockSpec((TIME_TILE, width), lambda b, t: (b * n_t + t, 0))
    prev_spec, next_spec = _halo_specs(n_t, time_block)
    y_dir = lambda d: pl.BlockSpec((1, TIME_TILE, D_A), lambda b, t: (d, b * n_t + t, 0))
    n_conv = TIME_TILE // conv_len
    return pl.pallas_call(
        functools.partial(_mixer_kernel, n_t=n_t, conv_len=conv_len),
        grid=(batch, n_t),
        in_specs=[
            tile(COLS_A), prev_spec, next_spec, tile(COLS_B), tile(COLS_C), y_dir(0), y_dir(1),
            _full((1, COLS_A)), _full((2, 1, D_A)), _full((2, 2 * R_ICLR, D_A)), _full((2, 1, D_A)),
            _full((R_GATE, D_A)), _full((1, D_A)), _full((1, D_A)), _full((1, D_A)), _full((D_A, D_A)),
            _full((1, D_B)), _full((1, D_B)), _full((N_GROUPS_B, GMLP_CHUNK, GMLP_CHUNK)),
            _full((GMLP_CHUNK, D_B)), _full((1, D_B)),
            _full((CONV_K, D_C)), _full((1, D_C)), _full((1, D_C)), _full((1, D_C)), _full((1, D_C)),
        ],
        out_specs=tile(D_MODEL),
        out_shape=jax.ShapeDtypeStruct((n, D_MODEL), BF16),
        scratch_shapes=[
            pltpu.VMEM((TIME_TILE + 2 * SUBLANES, COLS_A), F32),
            pltpu.VMEM((n_conv * (conv_len + 2 * CONV_HALO), D_C), F32),
        ],
        compiler_params=_params("arbitrary", "arbitrary"),
        name="token_mixers",
    )(pa, pa, pa, pb, pc, y_scan, y_scan,
      wts["mu_shift"], wts["a0"], wts["a2_pad"], wts["k_a"], wts["g2"], wts["r_k"], wts["lnx_g"], wts["lnx_b"],
      wts["head_blocks"], wts["gmlp_norm_g"], wts["gmlp_norm_b"], wts["gmlp_ws"], wts["gmlp_bias"],
      wts["beta_b"], wts["conv_w"], wts["conv_b"], wts["conv_norm_g"], wts["conv_norm_b"], wts["beta_c"])


def _top2_sum(a, b, c, d):
    hi1, lo1 = jnp.maximum(a, b), jnp.minimum(a, b)
    hi2, lo2 = jnp.maximum(c, d), jnp.minimum(c, d)
    return jnp.maximum(hi1, hi2) + jnp.maximum(jnp.minimum(hi1, hi2), jnp.maximum(lo1, lo2))


def _router_gates(logits_t, bias_ref):
    m = jnp.max(logits_t, axis=0, keepdims=True)
    e = jnp.exp(logits_t - m)
    probs = e / jnp.sum(e, axis=0, keepdims=True)
    sel = probs + bias_ref[...]
    p_row = [probs[i:i + 1, :] for i in range(N_EXPERTS)]
    s_row = [sel[i:i + 1, :] for i in range(N_EXPERTS)]
    best_val = None
    best = None
    for g in range(N_EXPERT_GROUPS):
        score = _top2_sum(*s_row[g * EXPERTS_PER_GROUP:(g + 1) * EXPERTS_PER_GROUP])
        if g == 0:
            best_val, best = score, jnp.zeros_like(score, dtype=jnp.int32)
        else:
            better = score > best_val
            best_val = jnp.where(better, score, best_val)
            best = jnp.where(better, g, best)
    chosen = []
    for i in range(N_EXPERTS):
        g = i // EXPERTS_PER_GROUP
        rank = jnp.zeros_like(best)
        for j in range(g * EXPERTS_PER_GROUP, (g + 1) * EXPERTS_PER_GROUP):
            if j == i:
                continue
            ahead = (s_row[j] >= s_row[i]) if j < i else (s_row[j] > s_row[i])
            rank = rank + ahead.astype(jnp.int32)
        chosen.append((best == g) & (rank < 2))
    picked = [jnp.where(chosen[i], p_row[i], 0.0) for i in range(N_EXPERTS)]
    denom = picked[0]
    for i in range(1, N_EXPERTS):
        denom = denom + picked[i]
    return jnp.concatenate([p / denom for p in picked], axis=0)


def _outproj_kernel(cat_ref, x_ref, mod_ref, wout_ref, g_ref, wr_hi_ref, wr_lo_ref, br_ref,
                    x1_ref, h2_ref, gates_ref):
    mod = mod_ref[0]
    out = _dot(cat_ref[...], wout_ref[...])
    x1 = x_ref[...] + mod[:, 2 * D_MODEL:3 * D_MODEL] * out
    x1_ref[...] = x1
    h2 = _modulated_rmsnorm(x1, g_ref[...], mod[:, 3 * D_MODEL:4 * D_MODEL], mod[:, 4 * D_MODEL:5 * D_MODEL])
    h_hi, h_lo = _split_bf16(h2, 2)
    h2_ref[...] = h_hi
    logits_t = (_dot_nt(wr_hi_ref[...], h_hi) + _dot_nt(wr_hi_ref[...], h_lo) + _dot_nt(wr_lo_ref[...], h_hi))
    gates_ref[...] = _router_gates(logits_t, br_ref)


def _out_projection(cat, x, mod, w_out, norm_g, wr_hi, wr_lo, b_router, seq_len, per_batch):
    n = x.shape[0]
    row = lambda width: pl.BlockSpec((ROW_TILE, width), lambda i: (i, 0))
    return pl.pallas_call(
        _outproj_kernel,
        grid=(n // ROW_TILE,),
        in_specs=[row(D_MODEL), row(D_MODEL), _mod_spec(seq_len, ROW_TILE, per_batch),
                  _full((D_MODEL, D_MODEL)), _full((1, D_MODEL)),
                  _full((N_EXPERTS, D_MODEL)), _full((N_EXPERTS, D_MODEL)), _full((N_EXPERTS, 1))],
        out_specs=[row(D_MODEL), row(D_MODEL), pl.BlockSpec((N_EXPERTS, ROW_TILE), lambda i: (0, i))],
        out_shape=[jax.ShapeDtypeStruct((n, D_MODEL), F32), jax.ShapeDtypeStruct((n, D_MODEL), BF16),
                   jax.ShapeDtypeStruct((N_EXPERTS, n), F32)],
        compiler_params=_params("arbitrary"),
        name="out_projection_router",
    )(cat, x, mod, w_out, norm_g, wr_hi, wr_lo, b_router)


def _moe_kernel(h_ref, gates_ref, x1_ref, mod_ref, wgu_ref, wdn_ref, fg_ref, o_ref, acc_ref, *, final_norm):
    e = pl.program_id(1)

    @pl.when(e == 0)
    def _():
        acc_ref[...] = jnp.zeros_like(acc_ref)

    gates = gates_ref[...]
    lane = lax.broadcasted_iota(jnp.int32, gates.shape, 1)
    gate = jnp.sum(jnp.where(lane == e, gates, 0.0), axis=1, keepdims=True)
    gu = _dot(h_ref[...], wgu_ref[0])
    g = gu[:, :D_EXPERT]
    act = g * _sigmoid(g) * gu[:, D_EXPERT:] * gate
    acc_ref[...] += _dot(act.astype(BF16), wdn_ref[0])

    @pl.when(e == N_EXPERTS - 1)
    def _():
        x2 = x1_ref[...] + mod_ref[0][:, 5 * D_MODEL:6 * D_MODEL] * acc_ref[...]
        if final_norm:
            x2 = _rmsnorm(x2, fg_ref[...])
        o_ref[...] = x2


def _mixture_of_experts(h2, gates, x1, mod, w_gu, w_down, final_g, seq_len, per_batch, final_norm):
    n = x1.shape[0]
    row = lambda width: pl.BlockSpec((MOE_ROW_TILE, width), lambda i, e: (i, 0))
    return pl.pallas_call(
        functools.partial(_moe_kernel, final_norm=final_norm),
        grid=(n // MOE_ROW_TILE, N_EXPERTS),
        in_specs=[row(D_MODEL), row(N_EXPERTS), row(D_MODEL), _mod_spec(seq_len, MOE_ROW_TILE, per_batch),
                  pl.BlockSpec((1, D_MODEL, 2 * D_EXPERT), lambda i, e: (e, 0, 0)),
                  pl.BlockSpec((1, D_EXPERT, D_MODEL), lambda i, e: (e, 0, 0)),
                  _full((1, D_MODEL))],
        out_specs=row(D_MODEL),
        out_shape=jax.ShapeDtypeStruct((n, D_MODEL), F32),
        scratch_shapes=[pltpu.VMEM((MOE_ROW_TILE, D_MODEL), F32)],
        compiler_params=_params("arbitrary", "arbitrary"),
        name="mixture_of_experts",
    )(h2, gates, x1, mod, w_gu, w_down, final_g)


def _pad_low_rank(w, rank):
    z = jnp.zeros_like(w[0])
    return jnp.stack([jnp.concatenate([w[0], z], axis=0), jnp.concatenate([z, w[1]], axis=0)]).astype(BF16)


def kernel(x_prompt, x_sample, state_rwkv, c, c_ctx, norm1_g, norm2_g, ada_w, ada_b, w_in, mu_shift, w0, w2, a0, a2, k_k, k_a, g2, r_k, lnx_g, lnx_b, gmlp_norm_g, gmlp_norm_b, gmlp_ws, gmlp_bs, beta_b, conv_w, conv_b, conv_norm_g, conv_norm_b, beta_c, w_out, w_router, b_router, moe_w_gu, moe_w_down, final_g):
    batch_p, seq_p, _ = x_prompt.shape
    batch_s, seq_s, _ = x_sample.shape

    n_cond = 1 + batch_s
    cond_rows = -(-n_cond // SUBLANES) * SUBLANES
    c_all = jnp.concatenate([c_ctx[None, :], c, jnp.zeros((cond_rows - n_cond, D_MODEL), F32)], axis=0)
    mod_all = _modulation(c_all, ada_w, ada_b)

    head_id = jnp.arange(D_A) // HEAD_DIM
    head_blocks = (head_id[:, None] == head_id[None, :]).astype(BF16)
    wr_t = w_router.T
    wr_hi = wr_t.astype(BF16)
    wr_lo = (wr_t - wr_hi.astype(F32)).astype(BF16)
    b_r = b_router.reshape(N_EXPERTS, 1)
    final_g2 = final_g.reshape(1, D_MODEL)

    groups = [
        dict(x=x_prompt.reshape(batch_p * seq_p, D_MODEL), batch=batch_p, seq=seq_p, conv_len=seq_p,
             per_batch=False, s0=None, mod_rows=slice(0, 1)),
        dict(x=x_sample.reshape(batch_s * seq_s, D_MODEL), batch=batch_s, seq=seq_s, conv_len=GRID_W,
             per_batch=True, s0=state_rwkv, mod_rows=slice(1, 1 + batch_s)),
    ]
    ctx_states = []
    outputs = []
    for grp in groups:
        x = grp["x"]
        batch, seq, per_batch = grp["batch"], grp["seq"], grp["per_batch"]
        for l in range(DEPTH):
            mod = mod_all[l, grp["mod_rows"]][:, None, :]
            wts = dict(
                mu_shift=mu_shift[l].reshape(1, COLS_A),
                w0=w0[l].reshape(2, 1, D_A), w2_pad=_pad_low_rank(w2[l], R_DECAY),
                a0=a0[l].reshape(2, 1, D_A), a2_pad=_pad_low_rank(a2[l], R_ICLR),
                k_k=k_k[l].reshape(2, 1, D_A), k_a=k_a[l].reshape(2, 1, D_A),
                g2=g2[l].astype(BF16), r_k=r_k[l].reshape(1, D_A),
                lnx_g=lnx_g[l].reshape(1, D_A), lnx_b=lnx_b[l].reshape(1, D_A), head_blocks=head_blocks,
                gmlp_norm_g=gmlp_norm_g[l].reshape(1, D_B), gmlp_norm_b=gmlp_norm_b[l].reshape(1, D_B),
                gmlp_ws=gmlp_ws[l].astype(BF16), gmlp_bias=jnp.repeat(gmlp_bs[l].T, HEAD_DIM, axis=1),
                beta_b=beta_b[l].reshape(1, D_B),
                conv_w=conv_w[l], conv_b=conv_b[l].reshape(1, D_C),
                conv_norm_g=conv_norm_g[l].reshape(1, D_C), conv_norm_b=conv_norm_b[l].reshape(1, D_C),
                beta_c=beta_c[l].reshape(1, D_C),
            )
            pa, pb, pc = _in_projection(x, mod, norm1_g[l].reshape(1, D_MODEL), w_in[l].astype(BF16), seq, per_batch)
            if grp["s0"] is None:
                s0 = jnp.zeros((batch, 2, N_HEADS_A, HEAD_DIM, HEAD_DIM), F32)
            else:
                s0 = grp["s0"][:, l]
            y_scan, s_new = _rwkv_scan(pa, s0, wts, batch, seq)
            cat = _token_mixers(pa, pb, pc, y_scan, wts, batch, seq, grp["conv_len"])
            x1, h2, gates_t = _out_projection(cat, x, mod, w_out[l].astype(BF16), norm2_g[l].reshape(1, D_MODEL),
                                              wr_hi, wr_lo, b_r, seq, per_batch)
            x = _mixture_of_experts(h2, gates_t.T, x1, mod, moe_w_gu[l].astype(BF16), moe_w_down[l].astype(BF16),
                                    final_g2, seq, per_batch, final_norm=(l == DEPTH - 1))
            if grp["s0"] is None:
                ctx_states.append(s_new)
        outputs.append(x.reshape(batch, seq, D_MODEL))
    return (outputs[0], outputs[1], jnp.stack(ctx_states, axis=1))
```

```python
import functools
import math

import jax
import jax.numpy as jnp
from jax import lax
from jax.experimental import pallas as pl
from jax.experimental.pallas import tpu as pltpu

D_MODEL = 1024
DEPTH = 2
GRID_W = 64
HEAD_DIM = 64
D_A = 512
N_HEADS_A = 8
R_DECAY = 64
R_ICLR = 64
R_GATE = 128
D_B = 256
N_GROUPS_B = 4
GMLP_CHUNK = 128
D_C = 256
CONV_K = 31
CONV_PAD = CONV_K // 2
N_EXPERTS = 16
N_EXPERT_GROUPS = 4
EXPERTS_PER_GROUP = 4
D_EXPERT = 256
NORM_EPS = 1e-6
LN_EPS = 1e-5
LN_X_EPS = 64e-5
COLS_A = 3 * D_A + 2 * R_DECAY + 2 * R_ICLR + R_GATE
COLS_B = 2 * D_B
COLS_C = 2 * D_C
D_PROJ = COLS_A + COLS_B + COLS_C
COL_WD = 3 * D_A
COL_AD = COL_WD + 2 * R_DECAY
COL_GD = COL_AD + 2 * R_ICLR

SUBLANES = 8
LANES = 128
TIME_TILE = 256
SCAN_CHUNK = 64
HEADS_PER_PAIR = LANES // HEAD_DIM
N_PAIRS = N_HEADS_A // HEADS_PER_PAIR
CONV_HALO = 16
ROW_TILE = 512
SHIFT_COL_CHUNK = 256
MOE_ROW_TILE = 1024
MOD_COL_TILE = 1536
VMEM_LIMIT = 56 * 1024 * 1024

BF16 = jnp.bfloat16
F32 = jnp.float32

_NT = (((1,), (1,)), ((), ()))
_TN = (((0,), (0,)), ((), ()))


def _dot(a, b):
    return jnp.dot(a, b, preferred_element_type=F32)


def _dot_nt(a, b):
    return lax.dot_general(a, b, _NT, preferred_element_type=F32)


def _dot_tn(a, b):
    return lax.dot_general(a, b, _TN, preferred_element_type=F32)


def _split_bf16(x, parts):
    out = []
    for _ in range(parts):
        p = x.astype(BF16)
        out.append(p)
        x = x - p.astype(F32)
    return out


def _sigmoid(x):
    return 1.0 / (1.0 + jnp.exp(-x))


def _group_sum(x, bd):
    hi, lo = _split_bf16(x, 2)
    return _dot(hi, bd) + _dot(lo, bd)


def _full(shape):
    zeros = (0,) * len(shape)
    return pl.BlockSpec(shape, lambda *_: zeros)


def _params(*sem):
    return pltpu.CompilerParams(dimension_semantics=sem, vmem_limit_bytes=VMEM_LIMIT)


def _mod_kernel(c_ref, w_ref, b_ref, o_ref):
    c = c_ref[...]
    s = c * _sigmoid(c)
    o_ref[0] = _dot(s.astype(BF16), w_ref[0].astype(BF16)) + b_ref[0]


def _modulation(c_all, ada_w, ada_b):
    rows = c_all.shape[0]
    n_col = ada_w.shape[-1] // MOD_COL_TILE
    return pl.pallas_call(
        _mod_kernel,
        grid=(DEPTH, n_col),
        in_specs=[
            _full((rows, D_MODEL)),
            pl.BlockSpec((1, D_MODEL, MOD_COL_TILE), lambda l, j: (l, 0, j)),
            pl.BlockSpec((1, 1, MOD_COL_TILE), lambda l, j: (l, 0, j)),
        ],
        out_specs=pl.BlockSpec((1, rows, MOD_COL_TILE), lambda l, j: (l, 0, j)),
        out_shape=jax.ShapeDtypeStruct((DEPTH, rows, ada_w.shape[-1]), F32),
        compiler_params=_params("arbitrary", "arbitrary"),
        name="adaln_modulation",
    )(c_all, ada_w, ada_b.reshape(DEPTH, 1, -1))


def _modulated_rmsnorm(x, gain, shift, scale):
    y = x * lax.rsqrt(jnp.mean(x * x, axis=-1, keepdims=True) + NORM_EPS)
    return y * gain * (1.0 + scale) + shift


def _inproj_kernel(x_ref, xprev_ref, xnext_ref, mod_ref, g_ref, w_ref, mu_ref, pa_ref, pb_ref, pc_ref, pad_ref,
                   *, seq_len):
    tm = x_ref.shape[0]
    mod = mod_ref[0]
    norm = lambda x: _modulated_rmsnorm(x, g_ref[...], mod[:, 0:D_MODEL], mod[:, D_MODEL:2 * D_MODEL])
    h = norm(x_ref[...]).astype(BF16)
    h_halo = jnp.concatenate([norm(xprev_ref[...]), norm(xnext_ref[...])], axis=0).astype(BF16)
    pos = (pl.program_id(0) * tm + lax.broadcasted_iota(jnp.int32, (tm, 1), 0)) % seq_len
    at_start = pos == 0
    at_end = pos == seq_len - 1
    for start in range(0, COLS_A, SHIFT_COL_CHUNK):
        cols = slice(start, min(start + SHIFT_COL_CHUNK, COLS_A))
        pa = _dot(h, w_ref[:, cols])
        p_halo = _dot(h_halo, w_ref[:, cols])
        pad_ref[0:SUBLANES, cols] = p_halo[:SUBLANES]
        pad_ref[SUBLANES:SUBLANES + tm, cols] = pa
        pad_ref[SUBLANES + tm:2 * SUBLANES + tm, cols] = p_halo[SUBLANES:]
        prev = jnp.where(at_start, 0.0, pad_ref[SUBLANES - 1:SUBLANES - 1 + tm, cols])
        nxt = jnp.where(at_end, 0.0, pad_ref[SUBLANES + 1:SUBLANES + 1 + tm, cols])
        pa_ref[:, cols] = pa + mu_ref[:, cols] * (0.5 * (prev + nxt) - pa)
    pb_ref[...] = _dot(h, w_ref[:, COLS_A:COLS_A + COLS_B])
    pc_ref[...] = _dot(h, w_ref[:, COLS_A + COLS_B:])


def _mod_spec(seq_len, row_tile, per_batch):
    if per_batch:
        return pl.BlockSpec((1, 1, 6 * D_MODEL), lambda i, *_: (i * row_tile // seq_len, 0, 0))
    return pl.BlockSpec((1, 1, 6 * D_MODEL), lambda i, *_: (0, 0, 0))


def _in_projection(x, mod, norm_g, w_in, mu_shift, seq_len, per_batch):
    n = x.shape[0]
    per_tile = ROW_TILE // SUBLANES
    last_block = n // SUBLANES - 1
    row = lambda width: pl.BlockSpec((ROW_TILE, width), lambda i: (i, 0))
    prev_spec = pl.BlockSpec((SUBLANES, D_MODEL), lambda i: (jnp.maximum(i * per_tile - 1, 0), 0))
    next_spec = pl.BlockSpec((SUBLANES, D_MODEL), lambda i: (jnp.minimum((i + 1) * per_tile, last_block), 0))
    return pl.pallas_call(
        functools.partial(_inproj_kernel, seq_len=seq_len),
        grid=(n // ROW_TILE,),
        in_specs=[row(D_MODEL), prev_spec, next_spec, _mod_spec(seq_len, ROW_TILE, per_batch), _full((1, D_MODEL)),
                  _full((D_MODEL, D_PROJ)), _full((1, COLS_A))],
        out_specs=[row(COLS_A), row(COLS_B), row(COLS_C)],
        out_shape=[jax.ShapeDtypeStruct((n, COLS_A), F32), jax.ShapeDtypeStruct((n, COLS_B), F32),
                   jax.ShapeDtypeStruct((n, COLS_C), F32)],
        scratch_shapes=[pltpu.VMEM((ROW_TILE + 2 * SUBLANES, COLS_A), F32)],
        compiler_params=_params("arbitrary"),
        name="in_projection",
    )(x, x, x, mod, norm_g, w_in, mu_shift)


def _block_diag(x, same_head):
    return jnp.where(same_head, jnp.concatenate([x, x], axis=0), jnp.zeros((), x.dtype))


def _diag_blocks(x, first_head):
    return jnp.where(first_head, x[:HEAD_DIM], x[HEAD_DIM:])


def _scan_kernel(pa_ref, s0_ref, w0_ref, w2_ref, a0_ref, a2_ref, kk_ref, ka_ref, bd_ref, y_ref, sout_ref,
                 lw_s, kt_s, kn_s, b_s, state_s, g_s, sa_s, q_s, y0_s, pc_s, *, n_t):
    d = pl.program_id(0)
    t = pl.program_id(2)
    n_chunks = TIME_TILE // SCAN_CHUNK
    c_len = SCAN_CHUNK
    pair_cols = lambda p: slice(p * LANES, (p + 1) * LANES)

    @pl.when(t == 0)
    def _():
        for p in range(N_PAIRS):
            state_s[p] = jnp.concatenate([s0_ref[0, 0, HEADS_PER_PAIR * p + j] for j in range(HEADS_PER_PAIR)], axis=1)

    k = pa_ref[:, D_A:2 * D_A]
    wl = w0_ref[0] + _dot(jnp.tanh(pa_ref[:, COL_WD:COL_WD + 2 * R_DECAY]).astype(BF16), w2_ref[0])
    lw_s[...] = -math.exp(-0.5) * _sigmoid(wl)
    a = _sigmoid(a0_ref[0] + _dot(pa_ref[:, COL_AD:COL_AD + 2 * R_ICLR].astype(BF16), a2_ref[0]))
    kx = k * kk_ref[0]
    kn = kx * lax.rsqrt(jnp.maximum(_group_sum(kx * kx, bd_ref[...]), 1e-24))
    kt_s[...] = k * (1.0 + (a - 1.0) * ka_ref[0])
    kn_s[...] = kn
    b_s[...] = kn * a

    sign = 1 - 2 * d
    row = lax.broadcasted_iota(jnp.int32, (c_len, c_len), 0)
    col = lax.broadcasted_iota(jnp.int32, (c_len, c_len), 1)
    upto_bf = jnp.where((row - col) * sign >= 0, 1.0, 0.0).astype(BF16)
    row2 = lax.broadcasted_iota(jnp.int32, (c_len, LANES), 0)
    col2 = lax.broadcasted_iota(jnp.int32, (c_len, LANES), 1)
    tok2 = col2 % HEAD_DIM
    before = (row2 - tok2) * sign > 0
    upto = (row2 - tok2) * sign >= 0
    eye = jnp.where(row2 == tok2, 1.0, 0.0)
    first_head = col2 < HEAD_DIM
    rowb = lax.broadcasted_iota(jnp.int32, (LANES, LANES), 0)
    colb = lax.broadcasted_iota(jnp.int32, (LANES, LANES), 1)
    same_head = (rowb // HEAD_DIM) == (colb // HEAD_DIM)
    bdiag = lambda x: _block_diag(x, same_head)

    rh, kap, vv, bh, kh, bq, kq = [], [], [], [], [], [], []
    for c in range(n_chunks):
        rows = slice(c * c_len, (c + 1) * c_len)
        lwc = lw_s[rows, :]
        cum = sum(_dot(upto_bf, piece) for piece in _split_bf16(lwc, 3))
        tot = jnp.sum(lwc, axis=0, keepdims=True)
        e_neg = jnp.exp(-cum)
        e_rest = jnp.exp(tot - cum)
        kap_all = (kn_s[rows, :] * jnp.exp(cum - lwc)).astype(BF16)
        rh_all = pa_ref[rows, 0:D_A] * jnp.exp(cum)
        kt_c = kt_s[rows, :]
        b_c = b_s[rows, :]
        kh_all = (kt_c * e_neg).astype(BF16)
        bh_all = (b_c * e_neg).astype(BF16)
        kq_all = (kt_c * e_rest).astype(BF16)
        bq_all = (b_c * e_rest).astype(BF16)
        v_all = pa_ref[rows, 2 * D_A:3 * D_A].astype(BF16)
        pc_s[c] = jnp.exp(tot)
        for p in range(N_PAIRS):
            rh.append(rh_all[:, pair_cols(p)])
            kap.append(kap_all[:, pair_cols(p)])
            vv.append(v_all[:, pair_cols(p)])
            bh.append(bh_all[:, pair_cols(p)])
            kh.append(kh_all[:, pair_cols(p)])
            bq.append(bq_all[:, pair_cols(p)])
            kq.append(kq_all[:, pair_cols(p)])
    units = range(len(rh))
    both = [jnp.concatenate([kap[n], rh[n].astype(BF16)], axis=0) for n in units]
    xb = [_dot_nt(both[n], bdiag(bh[n])) for n in units]
    xk = [_dot_nt(both[n], bdiag(kh[n])) for n in units]
    l_b = [jnp.where(before, xb[n][:c_len], 0.0) for n in units]
    a_b = [jnp.where(upto, xb[n][c_len:], 0.0).astype(BF16) for n in units]
    lk_ak = [jnp.concatenate([jnp.where(before, xk[n][:c_len], 0.0), jnp.where(upto, xk[n][c_len:], 0.0)],
                             axis=0).astype(BF16) for n in units]
    lkv_akv = [_dot(lk_ak[n], bdiag(vv[n])) for n in units]
    lkv = [lkv_akv[n][:c_len].astype(BF16) for n in units]
    akv = [lkv_akv[n][c_len:] for n in units]
    inv = [eye - l_b[n] for n in units]
    lp = [l_b[n].astype(BF16) for n in units]
    lp = [_dot(lp[n], bdiag(lp[n])).astype(BF16) for n in units]
    n_sq = int(math.log2(c_len)) - 1
    for j in range(n_sq):
        if j < n_sq - 1:
            res = [_dot(jnp.concatenate([inv[n].astype(BF16), lp[n]], axis=0), bdiag(lp[n])) for n in units]
            inv = [inv[n] + res[n][:c_len] for n in units]
            lp = [res[n][c_len:].astype(BF16) for n in units]
        else:
            inv = [inv[n] + _dot(inv[n].astype(BF16), bdiag(lp[n])) for n in units]
    inv_bf = [inv[n].astype(BF16) for n in units]
    wu = [_dot(inv_bf[n], jnp.concatenate([bdiag(kap[n]), bdiag(lkv[n])], axis=1)).astype(BF16) for n in units]
    w = [wu[n][:, :LANES] for n in units]
    u = [wu[n][:, LANES:] for n in units]
    ab_wu = [_dot(a_b[n], jnp.concatenate([bdiag(w[n]), bdiag(u[n])], axis=1)) for n in units]
    g = [_diag_blocks(_dot_tn(w[n], bq[n]), first_head) for n in units]
    sa = [_diag_blocks(_dot_tn(jnp.concatenate([vv[n], -u[n]], axis=0),
                               jnp.concatenate([kq[n], bq[n]], axis=0)), first_head) for n in units]
    for n in units:
        c, p = n // N_PAIRS, n % N_PAIRS
        q_s[c, p] = rh[n] - ab_wu[n][:, :LANES]
        y0_s[c, p] = akv[n] - ab_wu[n][:, LANES:]
        g_s[c, p] = g[n]
        sa_s[c, p] = sa[n]

    def advance(i, carry):
        c = i + d * (n_chunks - 1 - 2 * i)
        rows = pl.ds(pl.multiple_of(c * c_len, c_len), c_len)
        pc = pc_s[c]
        pairs = range(N_PAIRS)
        s = [state_s[p] for p in pairs]
        s_bf = [s[p].astype(BF16) for p in pairs]
        sg = [_dot(s_bf[p], bdiag(g_s[c, p].astype(BF16))) for p in pairs]
        qs = [_dot_nt(q_s[c, p].astype(BF16), bdiag(s_bf[p])) for p in pairs]
        for p in pairs:
            state_s[p] = s[p] * pc[:, pair_cols(p)] - sg[p] + sa_s[c, p]
            y_ref[0, rows, pair_cols(p)] = qs[p] + y0_s[c, p]
        return carry

    lax.fori_loop(0, n_chunks, advance, 0, unroll=True)

    @pl.when(t == n_t - 1)
    def _():
        for p in range(N_PAIRS):
            s = state_s[p]
            for j in range(HEADS_PER_PAIR):
                sout_ref[0, 0, HEADS_PER_PAIR * p + j] = s[:, j * HEAD_DIM:(j + 1) * HEAD_DIM]


def _rwkv_scan(pa, s0, wts, batch, seq_len):
    n = pa.shape[0]
    n_t = seq_len // TIME_TILE
    n_chunks = TIME_TILE // SCAN_CHUNK

    def tile_map(d, b, t):
        return (b * n_t + t + d * (n_t - 1 - 2 * t), 0)

    per_dir = lambda shape: pl.BlockSpec((1,) + shape, lambda d, b, t: (d,) + (0,) * len(shape))
    state_spec = pl.BlockSpec((1, 1, N_HEADS_A, HEAD_DIM, HEAD_DIM), lambda d, b, t: (b, d, 0, 0, 0))
    pair_mat = pltpu.VMEM((n_chunks, N_PAIRS, SCAN_CHUNK, LANES), F32)
    stream = pltpu.VMEM((TIME_TILE, D_A), F32)
    return pl.pallas_call(
        functools.partial(_scan_kernel, n_t=n_t),
        grid=(2, batch, n_t),
        in_specs=[
            pl.BlockSpec((TIME_TILE, COLS_A), tile_map), state_spec,
            per_dir((1, D_A)), per_dir((2 * R_DECAY, D_A)), per_dir((1, D_A)), per_dir((2 * R_ICLR, D_A)),
            per_dir((1, D_A)), per_dir((1, D_A)),
            _full((D_A, D_A)),
        ],
        out_specs=[pl.BlockSpec((1, TIME_TILE, D_A), lambda d, b, t: (d,) + tile_map(d, b, t)), state_spec],
        out_shape=[jax.ShapeDtypeStruct((2, n, D_A), F32),
                   jax.ShapeDtypeStruct((batch, 2, N_HEADS_A, HEAD_DIM, HEAD_DIM), F32)],
        scratch_shapes=[
            stream, stream, stream, stream,
            pltpu.VMEM((N_PAIRS, HEAD_DIM, LANES), F32),
            pair_mat, pair_mat, pair_mat, pair_mat,
            pltpu.VMEM((n_chunks, 1, D_A), F32),
        ],
        compiler_params=_params("arbitrary", "arbitrary", "arbitrary"),
        name="rwkv7_scan",
    )(pa, s0, wts["w0"], wts["w2_pad"], wts["a0"], wts["a2_pad"], wts["k_k"], wts["k_a"], wts["head_blocks"])


def _layernorm(x, g, b):
    mu = jnp.mean(x, axis=-1, keepdims=True)
    xc = x - mu
    var = jnp.mean(xc * xc, axis=-1, keepdims=True)
    return xc * lax.rsqrt(var + LN_EPS) * g + b


def _rmsnorm(x, g):
    return x * lax.rsqrt(jnp.mean(x * x, axis=-1, keepdims=True) + NORM_EPS) * g


def _mixer_kernel(pa_ref, pb_ref, pc_ref, yf_ref, yb_ref,
                  a0_ref, a2_ref, ka_ref, g2_ref, rk_ref, lnxg_ref, lnxb_ref, bd_ref,
                  gng_ref, gnb_ref, ws_ref, bs_ref, betab_ref,
                  cw_ref, cb_ref, cng_ref, cnb_ref, betac_ref,
                  cat_ref, cpad_ref, *, conv_len):
    bd = bd_ref[...]

    r = pa_ref[:, 0:D_A]
    k = pa_ref[:, D_A:2 * D_A]
    v = pa_ref[:, 2 * D_A:3 * D_A]
    ad = pa_ref[:, COL_AD:COL_AD + 2 * R_ICLR].astype(BF16)
    gd = pa_ref[:, COL_GD:COL_GD + R_GATE]
    a_f = _sigmoid(a0_ref[0] + _dot(ad, a2_ref[0]))
    a_b = _sigmoid(a0_ref[1] + _dot(ad, a2_ref[1]))
    kt_sum = k * (2.0 + (a_f - 1.0) * ka_ref[0] + (a_b - 1.0) * ka_ref[1])
    bonus = _group_sum(r * kt_sum * rk_ref[...], bd) * v
    gate = _dot(_sigmoid(gd).astype(BF16), g2_ref[...])
    y = yf_ref[0] + yb_ref[0]
    mean = _group_sum(y, bd) * (1.0 / HEAD_DIM)
    yc = y - mean
    var = _group_sum(yc * yc, bd) * (1.0 / HEAD_DIM)
    yn = yc * lax.rsqrt(var + LN_X_EPS) * lnxg_ref[...] + lnxb_ref[...]
    cat_ref[:, 0:D_A] = ((yn + bonus) * gate).astype(BF16)

    gb = jax.nn.gelu(pb_ref[...])
    u = gb[:, :D_B]
    vg = _layernorm(gb[:, D_B:], gng_ref[...], gnb_ref[...]).astype(BF16)
    head_of_lane = lax.broadcasted_iota(jnp.int32, (GMLP_CHUNK, D_B), 1) // HEAD_DIM
    sv_chunks = []
    for n in range(TIME_TILE // GMLP_CHUNK):
        vgc = vg[n * GMLP_CHUNK:(n + 1) * GMLP_CHUNK]
        sv = bs_ref[...]
        for g in range(N_GROUPS_B):
            sv = sv + jnp.where(head_of_lane == g, _dot(ws_ref[g], vgc), 0.0)
        sv_chunks.append(sv)
    sv = jnp.concatenate(sv_chunks, axis=0)
    cat_ref[:, D_A:D_A + D_B] = _rmsnorm(u * sv, betab_ref[...]).astype(BF16)

    pc = pc_ref[...]
    gl = pc[:, :D_C] * _sigmoid(pc[:, D_C:])
    stride = conv_len + 2 * CONV_HALO
    zeros = jnp.zeros((CONV_HALO, D_C), F32)
    convs = []
    for q in range(TIME_TILE // conv_len):
        base = q * stride
        cpad_ref[base:base + CONV_HALO, :] = zeros
        cpad_ref[base + CONV_HALO:base + CONV_HALO + conv_len, :] = gl[q * conv_len:(q + 1) * conv_len]
        cpad_ref[base + CONV_HALO + conv_len:base + stride, :] = zeros
    for q in range(TIME_TILE // conv_len):
        base = q * stride + CONV_HALO - CONV_PAD
        acc = jnp.zeros((conv_len, D_C), F32) + cb_ref[...]
        for j in range(CONV_K):
            acc = acc + cw_ref[j:j + 1, :] * cpad_ref[base + j:base + j + conv_len, :]
        convs.append(acc)
    conv = jnp.concatenate(convs, axis=0) if len(convs) > 1 else convs[0]
    z = _layernorm(conv, cng_ref[...], cnb_ref[...])
    z = z * _sigmoid(z)
    cat_ref[:, D_A + D_B:] = _rmsnorm(z, betac_ref[...]).astype(BF16)


def _token_mixers(pa, pb, pc, y_scan, wts, conv_len):
    n = pa.shape[0]
    tile = lambda width: pl.BlockSpec((TIME_TILE, width), lambda i: (i, 0))
    y_dir = lambda d: pl.BlockSpec((1, TIME_TILE, D_A), lambda i: (d, i, 0))
    n_conv = TIME_TILE // conv_len
    return pl.pallas_call(
        functools.partial(_mixer_kernel, conv_len=conv_len),
        grid=(n // TIME_TILE,),
        in_specs=[
            tile(COLS_A), tile(COLS_B), tile(COLS_C), y_dir(0), y_dir(1),
            _full((2, 1, D_A)), _full((2, 2 * R_ICLR, D_A)), _full((2, 1, D_A)),
            _full((R_GATE, D_A)), _full((1, D_A)), _full((1, D_A)), _full((1, D_A)), _full((D_A, D_A)),
            _full((1, D_B)), _full((1, D_B)), _full((N_GROUPS_B, GMLP_CHUNK, GMLP_CHUNK)),
            _full((GMLP_CHUNK, D_B)), _full((1, D_B)),
            _full((CONV_K, D_C)), _full((1, D_C)), _full((1, D_C)), _full((1, D_C)), _full((1, D_C)),
        ],
        out_specs=tile(D_MODEL),
        out_shape=jax.ShapeDtypeStruct((n, D_MODEL), BF16),
        scratch_shapes=[pltpu.VMEM((n_conv * (conv_len + 2 * CONV_HALO), D_C), F32)],
        compiler_params=_params("arbitrary"),
        name="token_mixers",
    )(pa, pb, pc, y_scan, y_scan,
      wts["a0"], wts["a2_pad"], wts["k_a"], wts["g2"], wts["r_k"], wts["lnx_g"], wts["lnx_b"],
      wts["head_blocks"], wts["gmlp_norm_g"], wts["gmlp_norm_b"], wts["gmlp_ws"], wts["gmlp_bias"],
      wts["beta_b"], wts["conv_w"], wts["conv_b"], wts["conv_norm_g"], wts["conv_norm_b"], wts["beta_c"])


def _top2_sum(a, b, c, d):
    hi1, lo1 = jnp.maximum(a, b), jnp.minimum(a, b)
    hi2, lo2 = jnp.maximum(c, d), jnp.minimum(c, d)
    return jnp.maximum(hi1, hi2) + jnp.maximum(jnp.minimum(hi1, hi2), jnp.maximum(lo1, lo2))


def _router_gates(logits_t, bias_ref):
    m = jnp.max(logits_t, axis=0, keepdims=True)
    e = jnp.exp(logits_t - m)
    probs = e / jnp.sum(e, axis=0, keepdims=True)
    sel = probs + bias_ref[...]
    p_row = [probs[i:i + 1, :] for i in range(N_EXPERTS)]
    s_row = [sel[i:i + 1, :] for i in range(N_EXPERTS)]
    best_val = None
    best = None
    for g in range(N_EXPERT_GROUPS):
        score = _top2_sum(*s_row[g * EXPERTS_PER_GROUP:(g + 1) * EXPERTS_PER_GROUP])
        if g == 0:
            best_val, best = score, jnp.zeros_like(score, dtype=jnp.int32)
        else:
            better = score > best_val
            best_val = jnp.where(better, score, best_val)
            best = jnp.where(better, g, best)
    chosen = []
    for i in range(N_EXPERTS):
        g = i // EXPERTS_PER_GROUP
        rank = jnp.zeros_like(best)
        for j in range(g * EXPERTS_PER_GROUP, (g + 1) * EXPERTS_PER_GROUP):
            if j == i:
                continue
            ahead = (s_row[j] >= s_row[i]) if j < i else (s_row[j] > s_row[i])
            rank = rank + ahead.astype(jnp.int32)
        chosen.append((best == g) & (rank < 2))
    picked = [jnp.where(chosen[i], p_row[i], 0.0) for i in range(N_EXPERTS)]
    denom = picked[0]
    for i in range(1, N_EXPERTS):
        denom = denom + picked[i]
    return jnp.concatenate([p / denom for p in picked], axis=0)


def _outproj_kernel(cat_ref, x_ref, mod_ref, wout_ref, g_ref, wr_hi_ref, wr_lo_ref, br_ref,
                    x1_ref, h2_ref, gates_ref):
    mod = mod_ref[0]
    out = _dot(cat_ref[...], wout_ref[...])
    x1 = x_ref[...] + mod[:, 2 * D_MODEL:3 * D_MODEL] * out
    x1_ref[...] = x1
    h2 = _modulated_rmsnorm(x1, g_ref[...], mod[:, 3 * D_MODEL:4 * D_MODEL], mod[:, 4 * D_MODEL:5 * D_MODEL])
    h_hi, h_lo = _split_bf16(h2, 2)
    h2_ref[...] = h_hi
    logits_t = (_dot_nt(wr_hi_ref[...], h_hi) + _dot_nt(wr_hi_ref[...], h_lo) + _dot_nt(wr_lo_ref[...], h_hi))
    gates_ref[...] = _router_gates(logits_t, br_ref)


def _out_projection(cat, x, mod, w_out, norm_g, wr_hi, wr_lo, b_router, seq_len, per_batch):
    n = x.shape[0]
    row = lambda width: pl.BlockSpec((ROW_TILE, width), lambda i: (i, 0))
    return pl.pallas_call(
        _outproj_kernel,
        grid=(n // ROW_TILE,),
        in_specs=[row(D_MODEL), row(D_MODEL), _mod_spec(seq_len, ROW_TILE, per_batch),
                  _full((D_MODEL, D_MODEL)), _full((1, D_MODEL)),
                  _full((N_EXPERTS, D_MODEL)), _full((N_EXPERTS, D_MODEL)), _full((N_EXPERTS, 1))],
        out_specs=[row(D_MODEL), row(D_MODEL), pl.BlockSpec((N_EXPERTS, ROW_TILE), lambda i: (0, i))],
        out_shape=[jax.ShapeDtypeStruct((n, D_MODEL), F32), jax.ShapeDtypeStruct((n, D_MODEL), BF16),
                   jax.ShapeDtypeStruct((N_EXPERTS, n), F32)],
        compiler_params=_params("arbitrary"),
        name="out_projection_router",
    )(cat, x, mod, w_out, norm_g, wr_hi, wr_lo, b_router)


def _moe_kernel(h_ref, gates_ref, x1_ref, mod_ref, wgu_ref, wdn_ref, fg_ref, o_ref, acc_ref, *, final_norm):
    e = pl.program_id(1)

    @pl.when(e == 0)
    def _():
        acc_ref[...] = jnp.zeros_like(acc_ref)

    gates = gates_ref[...]
    lane = lax.broadcasted_iota(jnp.int32, gates.shape, 1)
    gate = jnp.sum(jnp.where(lane == e, gates, 0.0), axis=1, keepdims=True)
    gu = _dot(h_ref[...], wgu_ref[0])
    g = gu[:, :D_EXPERT]
    act = g * _sigmoid(g) * gu[:, D_EXPERT:] * gate
    acc_ref[...] += _dot(act.astype(BF16), wdn_ref[0])

    @pl.when(e == N_EXPERTS - 1)
    def _():
        x2 = x1_ref[...] + mod_ref[0][:, 5 * D_MODEL:6 * D_MODEL] * acc_ref[...]
        if final_norm:
            x2 = _rmsnorm(x2, fg_ref[...])
        o_ref[...] = x2


def _mixture_of_experts(h2, gates, x1, mod, w_gu, w_down, final_g, seq_len, per_batch, final_norm):
    n = x1.shape[0]
    row = lambda width: pl.BlockSpec((MOE_ROW_TILE, width), lambda i, e: (i, 0))
    return pl.pallas_call(
        functools.partial(_moe_kernel, final_norm=final_norm),
        grid=(n // MOE_ROW_TILE, N_EXPERTS),
        in_specs=[row(D_MODEL), row(N_EXPERTS), row(D_MODEL), _mod_spec(seq_len, MOE_ROW_TILE, per_batch),
                  pl.BlockSpec((1, D_MODEL, 2 * D_EXPERT), lambda i, e: (e, 0, 0)),
                  pl.BlockSpec((1, D_EXPERT, D_MODEL), lambda i, e: (e, 0, 0)),
                  _full((1, D_MODEL))],
        out_specs=row(D_MODEL),
        out_shape=jax.ShapeDtypeStruct((n, D_MODEL), F32),
        scratch_shapes=[pltpu.VMEM((MOE_ROW_TILE, D_MODEL), F32)],
        compiler_params=_params("arbitrary", "arbitrary"),
        name="mixture_of_experts",
    )(h2, gates, x1, mod, w_gu, w_down, final_g)


def _pad_low_rank(w, rank):
    z = jnp.zeros_like(w[0])
    return jnp.stack([jnp.concatenate([w[0], z], axis=0), jnp.concatenate([z, w[1]], axis=0)]).astype(BF16)


def _layer_weights(l, head_blocks, mu_shift, w0, w2, a0, a2, k_k, k_a, g2, r_k, lnx_g, lnx_b, gmlp_norm_g,
                   gmlp_norm_b, gmlp_ws, gmlp_bs, beta_b, conv_w, conv_b, conv_norm_g, conv_norm_b, beta_c):
    return dict(
        mu_shift=mu_shift[l].reshape(1, COLS_A),
        w0=w0[l].reshape(2, 1, D_A), w2_pad=_pad_low_rank(w2[l], R_DECAY),
        a0=a0[l].reshape(2, 1, D_A), a2_pad=_pad_low_rank(a2[l], R_ICLR),
        k_k=k_k[l].reshape(2, 1, D_A), k_a=k_a[l].reshape(2, 1, D_A),
        g2=g2[l].astype(BF16), r_k=r_k[l].reshape(1, D_A),
        lnx_g=lnx_g[l].reshape(1, D_A), lnx_b=lnx_b[l].reshape(1, D_A), head_blocks=head_blocks,
        gmlp_norm_g=gmlp_norm_g[l].reshape(1, D_B), gmlp_norm_b=gmlp_norm_b[l].reshape(1, D_B),
        gmlp_ws=gmlp_ws[l].astype(BF16), gmlp_bias=jnp.repeat(gmlp_bs[l].T, HEAD_DIM, axis=1),
        beta_b=beta_b[l].reshape(1, D_B),
        conv_w=conv_w[l], conv_b=conv_b[l].reshape(1, D_C),
        conv_norm_g=conv_norm_g[l].reshape(1, D_C), conv_norm_b=conv_norm_b[l].reshape(1, D_C),
        beta_c=beta_c[l].reshape(1, D_C),
    )


def kernel(x_prompt, x_sample, state_rwkv, c, c_ctx, norm1_g, norm2_g, ada_w, ada_b, w_in, mu_shift, w0, w2, a0, a2, k_k, k_a, g2, r_k, lnx_g, lnx_b, gmlp_norm_g, gmlp_norm_b, gmlp_ws, gmlp_bs, beta_b, conv_w, conv_b, conv_norm_g, conv_norm_b, beta_c, w_out, w_router, b_router, moe_w_gu, moe_w_down, final_g):
    batch_p, seq_p, _ = x_prompt.shape
    batch_s, seq_s, _ = x_sample.shape

    n_cond = 1 + batch_s
    cond_rows = -(-n_cond // SUBLANES) * SUBLANES
    c_all = jnp.concatenate([c_ctx[None, :], c, jnp.zeros((cond_rows - n_cond, D_MODEL), F32)], axis=0)
    mod_all = _modulation(c_all, ada_w, ada_b)

    head_id = jnp.arange(D_A) // HEAD_DIM
    head_blocks = (head_id[:, None] == head_id[None, :]).astype(BF16)
    wr_t = w_router.T
    wr_hi = wr_t.astype(BF16)
    wr_lo = (wr_t - wr_hi.astype(F32)).astype(BF16)
    b_r = b_router.reshape(N_EXPERTS, 1)
    final_g2 = final_g.reshape(1, D_MODEL)
    layer_wts = [_layer_weights(l, head_blocks, mu_shift, w0, w2, a0, a2, k_k, k_a, g2, r_k, lnx_g, lnx_b,
                                gmlp_norm_g, gmlp_norm_b, gmlp_ws, gmlp_bs, beta_b, conv_w, conv_b, conv_norm_g,
                                conv_norm_b, beta_c) for l in range(DEPTH)]
    w_in_bf = w_in.astype(BF16)
    w_out_bf = w_out.astype(BF16)
    w_gu_bf = moe_w_gu.astype(BF16)
    w_down_bf = moe_w_down.astype(BF16)

    groups = [
        dict(x=x_prompt.reshape(batch_p * seq_p, D_MODEL), batch=batch_p, seq=seq_p, conv_len=seq_p,
             per_batch=False, s0=None, mod_rows=slice(0, 1)),
        dict(x=x_sample.reshape(batch_s * seq_s, D_MODEL), batch=batch_s, seq=seq_s, conv_len=GRID_W,
             per_batch=True, s0=state_rwkv, mod_rows=slice(1, 1 + batch_s)),
    ]
    ctx_states = []
    outputs = []
    for grp in groups:
        x = grp["x"]
        batch, seq, per_batch = grp["batch"], grp["seq"], grp["per_batch"]
        for l in range(DEPTH):
            mod = mod_all[l, grp["mod_rows"]][:, None, :]
            wts = layer_wts[l]
            pa, pb, pc = _in_projection(x, mod, norm1_g[l].reshape(1, D_MODEL), w_in_bf[l], wts["mu_shift"], seq,
                                        per_batch)
            if grp["s0"] is None:
                s0 = jnp.zeros((batch, 2, N_HEADS_A, HEAD_DIM, HEAD_DIM), F32)
            else:
                s0 = grp["s0"][:, l]
            y_scan, s_new = _rwkv_scan(pa, s0, wts, batch, seq)
            cat = _token_mixers(pa, pb, pc, y_scan, wts, grp["conv_len"])
            x1, h2, gates_t = _out_projection(cat, x, mod, w_out_bf[l], norm2_g[l].reshape(1, D_MODEL),
                                              wr_hi, wr_lo, b_r, seq, per_batch)
            x = _mixture_of_experts(h2, gates_t.T, x1, mod, w_gu_bf[l], w_down_bf[l],
                                    final_g2, seq, per_batch, final_norm=(l == DEPTH - 1))
            if grp["s0"] is None:
                ctx_states.append(s_new)
        outputs.append(x.reshape(batch, seq, D_MODEL))
    return (outputs[0], outputs[1], jnp.stack(ctx_states, axis=1))
```

```python
import functools
import math

import jax
import jax.numpy as jnp
from jax import lax
from jax.experimental import pallas as pl
from jax.experimental.pallas import tpu as pltpu

D_MODEL = 1024
DEPTH = 2
GRID_W = 64
HEAD_DIM = 64
D_A = 512
N_HEADS_A = 8
R_DECAY = 64
R_ICLR = 64
R_GATE = 128
D_B = 256
N_GROUPS_B = 4
GMLP_CHUNK = 128
D_C = 256
CONV_K = 31
CONV_PAD = CONV_K // 2
N_EXPERTS = 16
N_EXPERT_GROUPS = 4
EXPERTS_PER_GROUP = 4
D_EXPERT = 256
NORM_EPS = 1e-6
LN_EPS = 1e-5
LN_X_EPS = 64e-5
COLS_A = 3 * D_A + 2 * R_DECAY + 2 * R_ICLR + R_GATE
COLS_B = 2 * D_B
COLS_C = 2 * D_C
D_PROJ = COLS_A + COLS_B + COLS_C
COL_WD = 3 * D_A
COL_AD = COL_WD + 2 * R_DECAY
COL_GD = COL_AD + 2 * R_ICLR

SUBLANES = 8
LANES = 128
TIME_TILE = 256
SCAN_CHUNK = 64
HEADS_PER_PAIR = LANES // HEAD_DIM
N_PAIRS = N_HEADS_A // HEADS_PER_PAIR
CONV_HALO = 16
ROW_TILE = 512
SHIFT_COL_CHUNK = 256
MOE_ROW_TILE = 1024
MOE_CHUNK = 128
MOD_COL_TILE = 1536
VMEM_LIMIT = 56 * 1024 * 1024

BF16 = jnp.bfloat16
F32 = jnp.float32

_NT = (((1,), (1,)), ((), ()))
_TN = (((0,), (0,)), ((), ()))


def _dot(a, b):
    return jnp.dot(a, b, preferred_element_type=F32)


def _dot_nt(a, b):
    return lax.dot_general(a, b, _NT, preferred_element_type=F32)


def _dot_tn(a, b):
    return lax.dot_general(a, b, _TN, preferred_element_type=F32)


def _split_bf16(x, parts):
    out = []
    for _ in range(parts):
        p = x.astype(BF16)
        out.append(p)
        x = x - p.astype(F32)
    return out


def _sigmoid(x):
    return 1.0 / (1.0 + jnp.exp(-x))


def _group_sum(x, bd):
    hi, lo = _split_bf16(x, 2)
    return _dot(hi, bd) + _dot(lo, bd)


def _full(shape):
    zeros = (0,) * len(shape)
    return pl.BlockSpec(shape, lambda *_: zeros)


def _params(*sem):
    return pltpu.CompilerParams(dimension_semantics=sem, vmem_limit_bytes=VMEM_LIMIT)


def _mod_kernel(c_ref, w_ref, b_ref, o_ref):
    c = c_ref[...]
    s = c * _sigmoid(c)
    o_ref[0] = _dot(s.astype(BF16), w_ref[0].astype(BF16)) + b_ref[0]


def _modulation(c_all, ada_w, ada_b):
    rows = c_all.shape[0]
    n_col = ada_w.shape[-1] // MOD_COL_TILE
    return pl.pallas_call(
        _mod_kernel,
        grid=(DEPTH, n_col),
        in_specs=[
            _full((rows, D_MODEL)),
            pl.BlockSpec((1, D_MODEL, MOD_COL_TILE), lambda l, j: (l, 0, j)),
            pl.BlockSpec((1, 1, MOD_COL_TILE), lambda l, j: (l, 0, j)),
        ],
        out_specs=pl.BlockSpec((1, rows, MOD_COL_TILE), lambda l, j: (l, 0, j)),
        out_shape=jax.ShapeDtypeStruct((DEPTH, rows, ada_w.shape[-1]), F32),
        compiler_params=_params("arbitrary", "arbitrary"),
        name="adaln_modulation",
    )(c_all, ada_w, ada_b.reshape(DEPTH, 1, -1))


def _modulated_rmsnorm(x, gain, shift, scale):
    y = x * lax.rsqrt(jnp.mean(x * x, axis=-1, keepdims=True) + NORM_EPS)
    return y * gain * (1.0 + scale) + shift


def _inproj_kernel(x_ref, xprev_ref, xnext_ref, mod_ref, g_ref, w_ref, mu_ref, pa_ref, pb_ref, pc_ref, pad_ref,
                   *, seq_len):
    tm = x_ref.shape[0]
    mod = mod_ref[0]
    norm = lambda x: _modulated_rmsnorm(x, g_ref[...], mod[:, 0:D_MODEL], mod[:, D_MODEL:2 * D_MODEL])
    h = norm(x_ref[...]).astype(BF16)
    h_halo = jnp.concatenate([norm(xprev_ref[...]), norm(xnext_ref[...])], axis=0).astype(BF16)
    pos = (pl.program_id(0) * tm + lax.broadcasted_iota(jnp.int32, (tm, 1), 0)) % seq_len
    at_start = pos == 0
    at_end = pos == seq_len - 1
    for start in range(0, COLS_A, SHIFT_COL_CHUNK):
        cols = slice(start, min(start + SHIFT_COL_CHUNK, COLS_A))
        pa = _dot(h, w_ref[:, cols])
        p_halo = _dot(h_halo, w_ref[:, cols])
        pad_ref[0:SUBLANES, cols] = p_halo[:SUBLANES]
        pad_ref[SUBLANES:SUBLANES + tm, cols] = pa
        pad_ref[SUBLANES + tm:2 * SUBLANES + tm, cols] = p_halo[SUBLANES:]
        prev = jnp.where(at_start, 0.0, pad_ref[SUBLANES - 1:SUBLANES - 1 + tm, cols])
        nxt = jnp.where(at_end, 0.0, pad_ref[SUBLANES + 1:SUBLANES + 1 + tm, cols])
        pa_ref[:, cols] = pa + mu_ref[:, cols] * (0.5 * (prev + nxt) - pa)
    pb_ref[...] = _dot(h, w_ref[:, COLS_A:COLS_A + COLS_B])
    pc_ref[...] = _dot(h, w_ref[:, COLS_A + COLS_B:])


def _mod_spec(seq_len, row_tile, per_batch):
    if per_batch:
        return pl.BlockSpec((1, 1, 6 * D_MODEL), lambda i, *_: (i * row_tile // seq_len, 0, 0))
    return pl.BlockSpec((1, 1, 6 * D_MODEL), lambda i, *_: (0, 0, 0))


def _in_projection(x, mod, norm_g, w_in, mu_shift, seq_len, per_batch):
    n = x.shape[0]
    per_tile = ROW_TILE // SUBLANES
    last_block = n // SUBLANES - 1
    row = lambda width: pl.BlockSpec((ROW_TILE, width), lambda i: (i, 0))
    prev_spec = pl.BlockSpec((SUBLANES, D_MODEL), lambda i: (jnp.maximum(i * per_tile - 1, 0), 0))
    next_spec = pl.BlockSpec((SUBLANES, D_MODEL), lambda i: (jnp.minimum((i + 1) * per_tile, last_block), 0))
    return pl.pallas_call(
        functools.partial(_inproj_kernel, seq_len=seq_len),
        grid=(n // ROW_TILE,),
        in_specs=[row(D_MODEL), prev_spec, next_spec, _mod_spec(seq_len, ROW_TILE, per_batch), _full((1, D_MODEL)),
                  _full((D_MODEL, D_PROJ)), _full((1, COLS_A))],
        out_specs=[row(COLS_A), row(COLS_B), row(COLS_C)],
        out_shape=[jax.ShapeDtypeStruct((n, COLS_A), F32), jax.ShapeDtypeStruct((n, COLS_B), F32),
                   jax.ShapeDtypeStruct((n, COLS_C), F32)],
        scratch_shapes=[pltpu.VMEM((ROW_TILE + 2 * SUBLANES, COLS_A), F32)],
        compiler_params=_params("arbitrary"),
        name="in_projection",
    )(x, x, x, mod, norm_g, w_in, mu_shift)


def _block_diag(x, same_head):
    return jnp.where(same_head, jnp.concatenate([x, x], axis=0), jnp.zeros((), x.dtype))


def _diag_blocks(x, first_head):
    return jnp.where(first_head, x[:HEAD_DIM], x[HEAD_DIM:])


def _scan_kernel(pa_ref, s0_ref, w0_ref, w2_ref, a0_ref, a2_ref, kk_ref, ka_ref, bd_ref, y_ref, sout_ref,
                 lw_s, kt_s, kn_s, b_s, state_s, g_s, sa_s, q_s, y0_s, pc_s, *, n_t):
    d = pl.program_id(0)
    t = pl.program_id(2)
    n_chunks = TIME_TILE // SCAN_CHUNK
    c_len = SCAN_CHUNK
    pair_cols = lambda p: slice(p * LANES, (p + 1) * LANES)

    @pl.when(t == 0)
    def _():
        for p in range(N_PAIRS):
            state_s[p] = jnp.concatenate([s0_ref[0, 0, HEADS_PER_PAIR * p + j] for j in range(HEADS_PER_PAIR)], axis=1)

    k = pa_ref[:, D_A:2 * D_A]
    wl = w0_ref[0] + _dot(jnp.tanh(pa_ref[:, COL_WD:COL_WD + 2 * R_DECAY]).astype(BF16), w2_ref[0])
    lw_s[...] = -math.exp(-0.5) * _sigmoid(wl)
    a = _sigmoid(a0_ref[0] + _dot(pa_ref[:, COL_AD:COL_AD + 2 * R_ICLR].astype(BF16), a2_ref[0]))
    kx = k * kk_ref[0]
    kn = kx * lax.rsqrt(jnp.maximum(_group_sum(kx * kx, bd_ref[...]), 1e-24))
    kt_s[...] = k * (1.0 + (a - 1.0) * ka_ref[0])
    kn_s[...] = kn
    b_s[...] = kn * a

    sign = 1 - 2 * d
    row = lax.broadcasted_iota(jnp.int32, (c_len, c_len), 0)
    col = lax.broadcasted_iota(jnp.int32, (c_len, c_len), 1)
    upto_bf = jnp.where((row - col) * sign >= 0, 1.0, 0.0).astype(BF16)
    row2 = lax.broadcasted_iota(jnp.int32, (c_len, LANES), 0)
    col2 = lax.broadcasted_iota(jnp.int32, (c_len, LANES), 1)
    tok2 = col2 % HEAD_DIM
    before = (row2 - tok2) * sign > 0
    upto = (row2 - tok2) * sign >= 0
    eye = jnp.where(row2 == tok2, 1.0, 0.0)
    first_head = col2 < HEAD_DIM
    rowb = lax.broadcasted_iota(jnp.int32, (LANES, LANES), 0)
    colb = lax.broadcasted_iota(jnp.int32, (LANES, LANES), 1)
    same_head = (rowb // HEAD_DIM) == (colb // HEAD_DIM)
    bdiag = lambda x: _block_diag(x, same_head)

    rh, kap, vv, bh, kh, bq, kq = [], [], [], [], [], [], []
    for c in range(n_chunks):
        rows = slice(c * c_len, (c + 1) * c_len)
        lwc = lw_s[rows, :]
        cum = sum(_dot(upto_bf, piece) for piece in _split_bf16(lwc, 3))
        tot = jnp.sum(lwc, axis=0, keepdims=True)
        e_neg = jnp.exp(-cum)
        e_rest = jnp.exp(tot - cum)
        kap_all = (kn_s[rows, :] * jnp.exp(cum - lwc)).astype(BF16)
        rh_all = pa_ref[rows, 0:D_A] * jnp.exp(cum)
        kt_c = kt_s[rows, :]
        b_c = b_s[rows, :]
        kh_all = (kt_c * e_neg).astype(BF16)
        bh_all = (b_c * e_neg).astype(BF16)
        kq_all = (kt_c * e_rest).astype(BF16)
        bq_all = (b_c * e_rest).astype(BF16)
        v_all = pa_ref[rows, 2 * D_A:3 * D_A].astype(BF16)
        pc_s[c] = jnp.exp(tot)
        for p in range(N_PAIRS):
            rh.append(rh_all[:, pair_cols(p)])
            kap.append(kap_all[:, pair_cols(p)])
            vv.append(v_all[:, pair_cols(p)])
            bh.append(bh_all[:, pair_cols(p)])
            kh.append(kh_all[:, pair_cols(p)])
            bq.append(bq_all[:, pair_cols(p)])
            kq.append(kq_all[:, pair_cols(p)])
    units = range(len(rh))
    both = [jnp.concatenate([kap[n], rh[n].astype(BF16)], axis=0) for n in units]
    xb = [_dot_nt(both[n], bdiag(bh[n])) for n in units]
    xk = [_dot_nt(both[n], bdiag(kh[n])) for n in units]
    l_b = [jnp.where(before, xb[n][:c_len], 0.0) for n in units]
    a_b = [jnp.where(upto, xb[n][c_len:], 0.0).astype(BF16) for n in units]
    lk_ak = [jnp.concatenate([jnp.where(before, xk[n][:c_len], 0.0), jnp.where(upto, xk[n][c_len:], 0.0)],
                             axis=0).astype(BF16) for n in units]
    lkv_akv = [_dot(lk_ak[n], bdiag(vv[n])) for n in units]
    lkv = [lkv_akv[n][:c_len].astype(BF16) for n in units]
    akv = [lkv_akv[n][c_len:] for n in units]
    inv = [eye - l_b[n] for n in units]
    lp = [l_b[n].astype(BF16) for n in units]
    lp = [_dot(lp[n], bdiag(lp[n])).astype(BF16) for n in units]
    n_sq = int(math.log2(c_len)) - 1
    for j in range(n_sq):
        if j < n_sq - 1:
            res = [_dot(jnp.concatenate([inv[n].astype(BF16), lp[n]], axis=0), bdiag(lp[n])) for n in units]
            inv = [inv[n] + res[n][:c_len] for n in units]
            lp = [res[n][c_len:].astype(BF16) for n in units]
        else:
            inv = [inv[n] + _dot(inv[n].astype(BF16), bdiag(lp[n])) for n in units]
    inv_bf = [inv[n].astype(BF16) for n in units]
    wu = [_dot(inv_bf[n], jnp.concatenate([bdiag(kap[n]), bdiag(lkv[n])], axis=1)).astype(BF16) for n in units]
    w = [wu[n][:, :LANES] for n in units]
    u = [wu[n][:, LANES:] for n in units]
    ab_wu = [_dot(a_b[n], jnp.concatenate([bdiag(w[n]), bdiag(u[n])], axis=1)) for n in units]
    g = [_diag_blocks(_dot_tn(w[n], bq[n]), first_head) for n in units]
    sa = [_diag_blocks(_dot_tn(jnp.concatenate([vv[n], -u[n]], axis=0),
                               jnp.concatenate([kq[n], bq[n]], axis=0)), first_head) for n in units]
    for n in units:
        c, p = n // N_PAIRS, n % N_PAIRS
        q_s[c, p] = rh[n] - ab_wu[n][:, :LANES]
        y0_s[c, p] = akv[n] - ab_wu[n][:, LANES:]
        g_s[c, p] = g[n]
        sa_s[c, p] = sa[n]

    def advance(i, carry):
        c = i + d * (n_chunks - 1 - 2 * i)
        rows = pl.ds(pl.multiple_of(c * c_len, c_len), c_len)
        pc = pc_s[c]
        pairs = range(N_PAIRS)
        s = [state_s[p] for p in pairs]
        s_bf = [s[p].astype(BF16) for p in pairs]
        sg = [_dot(s_bf[p], bdiag(g_s[c, p].astype(BF16))) for p in pairs]
        qs = [_dot_nt(q_s[c, p].astype(BF16), bdiag(s_bf[p])) for p in pairs]
        for p in pairs:
            state_s[p] = s[p] * pc[:, pair_cols(p)] - sg[p] + sa_s[c, p]
            y_ref[0, rows, pair_cols(p)] = qs[p] + y0_s[c, p]
        return carry

    lax.fori_loop(0, n_chunks, advance, 0, unroll=True)

    @pl.when(t == n_t - 1)
    def _():
        for p in range(N_PAIRS):
            s = state_s[p]
            for j in range(HEADS_PER_PAIR):
                sout_ref[0, 0, HEADS_PER_PAIR * p + j] = s[:, j * HEAD_DIM:(j + 1) * HEAD_DIM]


def _rwkv_scan(pa, s0, wts, batch, seq_len):
    n = pa.shape[0]
    n_t = seq_len // TIME_TILE
    n_chunks = TIME_TILE // SCAN_CHUNK

    def tile_map(d, b, t):
        return (b * n_t + t + d * (n_t - 1 - 2 * t), 0)

    per_dir = lambda shape: pl.BlockSpec((1,) + shape, lambda d, b, t: (d,) + (0,) * len(shape))
    state_spec = pl.BlockSpec((1, 1, N_HEADS_A, HEAD_DIM, HEAD_DIM), lambda d, b, t: (b, d, 0, 0, 0))
    pair_mat = pltpu.VMEM((n_chunks, N_PAIRS, SCAN_CHUNK, LANES), F32)
    stream = pltpu.VMEM((TIME_TILE, D_A), F32)
    return pl.pallas_call(
        functools.partial(_scan_kernel, n_t=n_t),
        grid=(2, batch, n_t),
        in_specs=[
            pl.BlockSpec((TIME_TILE, COLS_A), tile_map), state_spec,
            per_dir((1, D_A)), per_dir((2 * R_DECAY, D_A)), per_dir((1, D_A)), per_dir((2 * R_ICLR, D_A)),
            per_dir((1, D_A)), per_dir((1, D_A)),
            _full((D_A, D_A)),
        ],
        out_specs=[pl.BlockSpec((1, TIME_TILE, D_A), lambda d, b, t: (d,) + tile_map(d, b, t)), state_spec],
        out_shape=[jax.ShapeDtypeStruct((2, n, D_A), F32),
                   jax.ShapeDtypeStruct((batch, 2, N_HEADS_A, HEAD_DIM, HEAD_DIM), F32)],
        scratch_shapes=[
            stream, stream, stream, stream,
            pltpu.VMEM((N_PAIRS, HEAD_DIM, LANES), F32),
            pair_mat, pair_mat, pair_mat, pair_mat,
            pltpu.VMEM((n_chunks, 1, D_A), F32),
        ],
        compiler_params=_params("arbitrary", "arbitrary", "arbitrary"),
        name="rwkv7_scan",
    )(pa, s0, wts["w0"], wts["w2_pad"], wts["a0"], wts["a2_pad"], wts["k_k"], wts["k_a"], wts["head_blocks"])


def _layernorm(x, g, b):
    mu = jnp.mean(x, axis=-1, keepdims=True)
    xc = x - mu
    var = jnp.mean(xc * xc, axis=-1, keepdims=True)
    return xc * lax.rsqrt(var + LN_EPS) * g + b


def _rmsnorm(x, g):
    return x * lax.rsqrt(jnp.mean(x * x, axis=-1, keepdims=True) + NORM_EPS) * g


def _mixer_kernel(pa_ref, pb_ref, pc_ref, yf_ref, yb_ref,
                  a0_ref, a2_ref, ka_ref, g2_ref, rk_ref, lnxg_ref, lnxb_ref, bd_ref,
                  gng_ref, gnb_ref, ws_ref, bs_ref, betab_ref,
                  cw_ref, cb_ref, cng_ref, cnb_ref, betac_ref,
                  cat_ref, cpad_ref, *, conv_len):
    bd = bd_ref[...]

    r = pa_ref[:, 0:D_A]
    k = pa_ref[:, D_A:2 * D_A]
    v = pa_ref[:, 2 * D_A:3 * D_A]
    ad = pa_ref[:, COL_AD:COL_AD + 2 * R_ICLR].astype(BF16)
    gd = pa_ref[:, COL_GD:COL_GD + R_GATE]
    a_f = _sigmoid(a0_ref[0] + _dot(ad, a2_ref[0]))
    a_b = _sigmoid(a0_ref[1] + _dot(ad, a2_ref[1]))
    kt_sum = k * (2.0 + (a_f - 1.0) * ka_ref[0] + (a_b - 1.0) * ka_ref[1])
    bonus = _group_sum(r * kt_sum * rk_ref[...], bd) * v
    gate = _dot(_sigmoid(gd).astype(BF16), g2_ref[...])
    y = yf_ref[0] + yb_ref[0]
    mean = _group_sum(y, bd) * (1.0 / HEAD_DIM)
    yc = y - mean
    var = _group_sum(yc * yc, bd) * (1.0 / HEAD_DIM)
    yn = yc * lax.rsqrt(var + LN_X_EPS) * lnxg_ref[...] + lnxb_ref[...]
    cat_ref[:, 0:D_A] = ((yn + bonus) * gate).astype(BF16)

    gb = jax.nn.gelu(pb_ref[...])
    u = gb[:, :D_B]
    vg = _layernorm(gb[:, D_B:], gng_ref[...], gnb_ref[...]).astype(BF16)
    head_of_lane = lax.broadcasted_iota(jnp.int32, (GMLP_CHUNK, D_B), 1) // HEAD_DIM
    sv_chunks = []
    for n in range(TIME_TILE // GMLP_CHUNK):
        vgc = vg[n * GMLP_CHUNK:(n + 1) * GMLP_CHUNK]
        sv = bs_ref[...]
        for g in range(N_GROUPS_B):
            sv = sv + jnp.where(head_of_lane == g, _dot(ws_ref[g], vgc), 0.0)
        sv_chunks.append(sv)
    sv = jnp.concatenate(sv_chunks, axis=0)
    cat_ref[:, D_A:D_A + D_B] = _rmsnorm(u * sv, betab_ref[...]).astype(BF16)

    pc = pc_ref[...]
    gl = pc[:, :D_C] * _sigmoid(pc[:, D_C:])
    stride = conv_len + 2 * CONV_HALO
    zeros = jnp.zeros((CONV_HALO, D_C), F32)
    convs = []
    for q in range(TIME_TILE // conv_len):
        base = q * stride
        cpad_ref[base:base + CONV_HALO, :] = zeros
        cpad_ref[base + CONV_HALO:base + CONV_HALO + conv_len, :] = gl[q * conv_len:(q + 1) * conv_len]
        cpad_ref[base + CONV_HALO + conv_len:base + stride, :] = zeros
    for q in range(TIME_TILE // conv_len):
        base = q * stride + CONV_HALO - CONV_PAD
        acc = jnp.zeros((conv_len, D_C), F32) + cb_ref[...]
        for j in range(CONV_K):
            acc = acc + cw_ref[j:j + 1, :] * cpad_ref[base + j:base + j + conv_len, :]
        convs.append(acc)
    conv = jnp.concatenate(convs, axis=0) if len(convs) > 1 else convs[0]
    z = _layernorm(conv, cng_ref[...], cnb_ref[...])
    z = z * _sigmoid(z)
    cat_ref[:, D_A + D_B:] = _rmsnorm(z, betac_ref[...]).astype(BF16)


def _token_mixers(pa, pb, pc, y_scan, wts, conv_len):
    n = pa.shape[0]
    tile = lambda width: pl.BlockSpec((TIME_TILE, width), lambda i: (i, 0))
    y_dir = lambda d: pl.BlockSpec((1, TIME_TILE, D_A), lambda i: (d, i, 0))
    n_conv = TIME_TILE // conv_len
    return pl.pallas_call(
        functools.partial(_mixer_kernel, conv_len=conv_len),
        grid=(n // TIME_TILE,),
        in_specs=[
            tile(COLS_A), tile(COLS_B), tile(COLS_C), y_dir(0), y_dir(1),
            _full((2, 1, D_A)), _full((2, 2 * R_ICLR, D_A)), _full((2, 1, D_A)),
            _full((R_GATE, D_A)), _full((1, D_A)), _full((1, D_A)), _full((1, D_A)), _full((D_A, D_A)),
            _full((1, D_B)), _full((1, D_B)), _full((N_GROUPS_B, GMLP_CHUNK, GMLP_CHUNK)),
            _full((GMLP_CHUNK, D_B)), _full((1, D_B)),
            _full((CONV_K, D_C)), _full((1, D_C)), _full((1, D_C)), _full((1, D_C)), _full((1, D_C)),
        ],
        out_specs=tile(D_MODEL),
        out_shape=jax.ShapeDtypeStruct((n, D_MODEL), BF16),
        scratch_shapes=[pltpu.VMEM((n_conv * (conv_len + 2 * CONV_HALO), D_C), F32)],
        compiler_params=_params("arbitrary"),
        name="token_mixers",
    )(pa, pb, pc, y_scan, y_scan,
      wts["a0"], wts["a2_pad"], wts["k_a"], wts["g2"], wts["r_k"], wts["lnx_g"], wts["lnx_b"],
      wts["head_blocks"], wts["gmlp_norm_g"], wts["gmlp_norm_b"], wts["gmlp_ws"], wts["gmlp_bias"],
      wts["beta_b"], wts["conv_w"], wts["conv_b"], wts["conv_norm_g"], wts["conv_norm_b"], wts["beta_c"])


def _top2_sum(a, b, c, d):
    hi1, lo1 = jnp.maximum(a, b), jnp.minimum(a, b)
    hi2, lo2 = jnp.maximum(c, d), jnp.minimum(c, d)
    return jnp.maximum(hi1, hi2) + jnp.maximum(jnp.minimum(hi1, hi2), jnp.maximum(lo1, lo2))


def _router_gates(logits_t, bias_ref):
    m = jnp.max(logits_t, axis=0, keepdims=True)
    e = jnp.exp(logits_t - m)
    probs = e / jnp.sum(e, axis=0, keepdims=True)
    sel = probs + bias_ref[...]
    p_row = [probs[i:i + 1, :] for i in range(N_EXPERTS)]
    s_row = [sel[i:i + 1, :] for i in range(N_EXPERTS)]
    best_val = None
    best = None
    for g in range(N_EXPERT_GROUPS):
        score = _top2_sum(*s_row[g * EXPERTS_PER_GROUP:(g + 1) * EXPERTS_PER_GROUP])
        if g == 0:
            best_val, best = score, jnp.zeros_like(score, dtype=jnp.int32)
        else:
            better = score > best_val
            best_val = jnp.where(better, score, best_val)
            best = jnp.where(better, g, best)
    chosen = []
    for i in range(N_EXPERTS):
        g = i // EXPERTS_PER_GROUP
        rank = jnp.zeros_like(best)
        for j in range(g * EXPERTS_PER_GROUP, (g + 1) * EXPERTS_PER_GROUP):
            if j == i:
                continue
            ahead = (s_row[j] >= s_row[i]) if j < i else (s_row[j] > s_row[i])
            rank = rank + ahead.astype(jnp.int32)
        chosen.append((best == g) & (rank < 2))
    picked = [jnp.where(chosen[i], p_row[i], 0.0) for i in range(N_EXPERTS)]
    denom = picked[0]
    for i in range(1, N_EXPERTS):
        denom = denom + picked[i]
    gates = jnp.concatenate([p / denom for p in picked], axis=0)
    group_rows = [jnp.where(best == g, 1.0, 0.0) for g in range(N_EXPERT_GROUPS)]
    group_rows += [jnp.zeros_like(group_rows[0])] * (SUBLANES - N_EXPERT_GROUPS)
    return gates, jnp.concatenate(group_rows, axis=0)


def _outproj_kernel(cat_ref, x_ref, mod_ref, wout_ref, g_ref, wr_hi_ref, wr_lo_ref, br_ref,
                    x1_ref, h2_ref, gates_ref, group_ref):
    mod = mod_ref[0]
    out = _dot(cat_ref[...], wout_ref[...])
    x1 = x_ref[...] + mod[:, 2 * D_MODEL:3 * D_MODEL] * out
    x1_ref[...] = x1
    h2 = _modulated_rmsnorm(x1, g_ref[...], mod[:, 3 * D_MODEL:4 * D_MODEL], mod[:, 4 * D_MODEL:5 * D_MODEL])
    h_hi, h_lo = _split_bf16(h2, 2)
    h2_ref[...] = h_hi
    logits_t = (_dot_nt(wr_hi_ref[...], h_hi) + _dot_nt(wr_hi_ref[...], h_lo) + _dot_nt(wr_lo_ref[...], h_hi))
    gates_ref[...], group_ref[...] = _router_gates(logits_t, br_ref)


def _out_projection(cat, x, mod, w_out, norm_g, wr_hi, wr_lo, b_router, seq_len, per_batch):
    n = x.shape[0]
    row = lambda width: pl.BlockSpec((ROW_TILE, width), lambda i: (i, 0))
    return pl.pallas_call(
        _outproj_kernel,
        grid=(n // ROW_TILE,),
        in_specs=[row(D_MODEL), row(D_MODEL), _mod_spec(seq_len, ROW_TILE, per_batch),
                  _full((D_MODEL, D_MODEL)), _full((1, D_MODEL)),
                  _full((N_EXPERTS, D_MODEL)), _full((N_EXPERTS, D_MODEL)), _full((N_EXPERTS, 1))],
        out_specs=[row(D_MODEL), row(D_MODEL), pl.BlockSpec((N_EXPERTS, ROW_TILE), lambda i: (0, i)),
                   pl.BlockSpec((SUBLANES, ROW_TILE), lambda i: (0, i))],
        out_shape=[jax.ShapeDtypeStruct((n, D_MODEL), F32), jax.ShapeDtypeStruct((n, D_MODEL), BF16),
                   jax.ShapeDtypeStruct((N_EXPERTS, n), F32), jax.ShapeDtypeStruct((SUBLANES, n), F32)],
        compiler_params=_params("arbitrary"),
        name="out_projection_router",
    )(cat, x, mod, w_out, norm_g, wr_hi, wr_lo, b_router)


def _moe_kernel(start_ref, h_ref, gates_ref, grp_t_ref, grp_c_ref, earlier_ref, later_ref, x1_ref, mod_ref, wgu_ref,
                wdn_ref, fg_ref, o_ref, xs_ref, gs_ref, ys_ref, pt_ref, *, final_norm):
    blk = pl.program_id(0)
    grp = pl.program_id(1)
    tm = h_ref.shape[0]

    @pl.when(grp == 0)
    def _():
        earlier = earlier_ref[...]
        grp_t = grp_t_ref[...]
        grp_c = grp_c_ref[...]
        rank_t = _dot(grp_t.astype(BF16), earlier)
        rank_c = _dot(later_ref[...], grp_c.astype(BF16))
        slot_t = jnp.zeros((1, tm), F32)
        slot_c = jnp.zeros((tm, 1), F32)
        for g in range(N_EXPERT_GROUPS):
            first = start_ref[blk, g].astype(F32)
            slot_t = slot_t + grp_t[g:g + 1, :] * (rank_t[g:g + 1, :] + first)
            slot_c = slot_c + grp_c[:, g:g + 1] * (rank_c[:, g:g + 1] + first)
        slot_rows = lax.broadcasted_iota(jnp.int32, (tm, tm), 0)
        slot_lanes = lax.broadcasted_iota(jnp.int32, (tm, tm), 1)
        to_slots = jnp.where(slot_rows == slot_t.astype(jnp.int32), 1.0, 0.0).astype(BF16)
        pt_ref[...] = jnp.where(slot_lanes == slot_c.astype(jnp.int32), 1.0, 0.0).astype(BF16)
        xs_ref[...] = _dot(to_slots, h_ref[...]).astype(BF16)
        pieces = _dot(to_slots, jnp.concatenate(_split_bf16(gates_ref[...], 3), axis=1))
        gs_ref[...] = (pieces[:, 0:N_EXPERTS] + pieces[:, N_EXPERTS:2 * N_EXPERTS]
                       + pieces[:, 2 * N_EXPERTS:3 * N_EXPERTS])
        ys_ref[...] = jnp.zeros_like(ys_ref)

    lo = start_ref[blk, grp] // MOE_CHUNK
    hi = (start_ref[blk, grp + 1] + MOE_CHUNK - 1) // MOE_CHUNK
    lane = lax.broadcasted_iota(jnp.int32, (MOE_CHUNK, N_EXPERTS), 1)

    def chunk(j, carry):
        rows = pl.ds(pl.multiple_of(j * MOE_CHUNK, MOE_CHUNK), MOE_CHUNK)
        x = xs_ref[rows, :]
        gates = gs_ref[rows, :]
        acts = []
        for e in range(EXPERTS_PER_GROUP):
            gate = jnp.sum(jnp.where(lane == grp * EXPERTS_PER_GROUP + e, gates, 0.0), axis=1, keepdims=True)
            gu = _dot(x, wgu_ref[0, e])
            g = gu[:, :D_EXPERT]
            acts.append((g * _sigmoid(g) * gu[:, D_EXPERT:] * gate).astype(BF16))
        ys_ref[rows, :] += _dot(jnp.concatenate(acts, axis=1), wdn_ref[0])
        return carry

    lax.fori_loop(lo, hi, chunk, 0)

    @pl.when(grp == N_EXPERT_GROUPS - 1)
    def _():
        moe = _dot(pt_ref[...], ys_ref[...].astype(BF16))
        x2 = x1_ref[...] + mod_ref[0][:, 5 * D_MODEL:6 * D_MODEL] * moe
        if final_norm:
            x2 = _rmsnorm(x2, fg_ref[...])
        o_ref[...] = x2


def _mixture_of_experts(h2, gates_t, group_t, x1, mod, w_gu, w_down, earlier, later, final_g, seq_len, per_batch,
                        final_norm):
    n = x1.shape[0]
    n_blocks = n // MOE_ROW_TILE
    counts = group_t[:N_EXPERT_GROUPS].reshape(N_EXPERT_GROUPS, n_blocks, MOE_ROW_TILE).sum(axis=-1)
    starts = jnp.concatenate([jnp.zeros((1, n_blocks), F32), jnp.cumsum(counts, axis=0)], axis=0)
    starts = starts.T.astype(jnp.int32)
    row = lambda width: pl.BlockSpec((MOE_ROW_TILE, width), lambda i, g, s: (i, 0))
    col = lambda height: pl.BlockSpec((height, MOE_ROW_TILE), lambda i, g, s: (0, i))
    order = pl.BlockSpec((MOE_ROW_TILE, MOE_ROW_TILE), lambda i, g, s: (0, 0))
    mod_spec = _mod_spec(seq_len, MOE_ROW_TILE, per_batch)
    grid_spec = pltpu.PrefetchScalarGridSpec(
        num_scalar_prefetch=1,
        grid=(n_blocks, N_EXPERT_GROUPS),
        in_specs=[row(D_MODEL), row(N_EXPERTS), col(SUBLANES), row(SUBLANES), order, order,
                  row(D_MODEL), mod_spec,
                  pl.BlockSpec((1, EXPERTS_PER_GROUP, D_MODEL, 2 * D_EXPERT), lambda i, g, s: (g, 0, 0, 0)),
                  pl.BlockSpec((1, EXPERTS_PER_GROUP * D_EXPERT, D_MODEL), lambda i, g, s: (g, 0, 0)),
                  pl.BlockSpec((1, D_MODEL), lambda i, g, s: (0, 0))],
        out_specs=row(D_MODEL),
        scratch_shapes=[pltpu.VMEM((MOE_ROW_TILE, D_MODEL), BF16), pltpu.VMEM((MOE_ROW_TILE, N_EXPERTS), F32),
                        pltpu.VMEM((MOE_ROW_TILE, D_MODEL), F32), pltpu.VMEM((MOE_ROW_TILE, MOE_ROW_TILE), BF16)],
    )
    return pl.pallas_call(
        functools.partial(_moe_kernel, final_norm=final_norm),
        grid_spec=grid_spec,
        out_shape=jax.ShapeDtypeStruct((n, D_MODEL), F32),
        compiler_params=_params("arbitrary", "arbitrary"),
        name="mixture_of_experts",
    )(starts, h2, gates_t.T, group_t, group_t.T, earlier, later, x1, mod, w_gu, w_down, final_g)


def _pad_low_rank(w, rank):
    z = jnp.zeros_like(w[0])
    return jnp.stack([jnp.concatenate([w[0], z], axis=0), jnp.concatenate([z, w[1]], axis=0)]).astype(BF16)


def _layer_weights(l, head_blocks, mu_shift, w0, w2, a0, a2, k_k, k_a, g2, r_k, lnx_g, lnx_b, gmlp_norm_g,
                   gmlp_norm_b, gmlp_ws, gmlp_bs, beta_b, conv_w, conv_b, conv_norm_g, conv_norm_b, beta_c):
    return dict(
        mu_shift=mu_shift[l].reshape(1, COLS_A),
        w0=w0[l].reshape(2, 1, D_A), w2_pad=_pad_low_rank(w2[l], R_DECAY),
        a0=a0[l].reshape(2, 1, D_A), a2_pad=_pad_low_rank(a2[l], R_ICLR),
        k_k=k_k[l].reshape(2, 1, D_A), k_a=k_a[l].reshape(2, 1, D_A),
        g2=g2[l].astype(BF16), r_k=r_k[l].reshape(1, D_A),
        lnx_g=lnx_g[l].reshape(1, D_A), lnx_b=lnx_b[l].reshape(1, D_A), head_blocks=head_blocks,
        gmlp_norm_g=gmlp_norm_g[l].reshape(1, D_B), gmlp_norm_b=gmlp_norm_b[l].reshape(1, D_B),
        gmlp_ws=gmlp_ws[l].astype(BF16), gmlp_bias=jnp.repeat(gmlp_bs[l].T, HEAD_DIM, axis=1),
        beta_b=beta_b[l].reshape(1, D_B),
        conv_w=conv_w[l], conv_b=conv_b[l].reshape(1, D_C),
        conv_norm_g=conv_norm_g[l].reshape(1, D_C), conv_norm_b=conv_norm_b[l].reshape(1, D_C),
        beta_c=beta_c[l].reshape(1, D_C),
    )


def kernel(x_prompt, x_sample, state_rwkv, c, c_ctx, norm1_g, norm2_g, ada_w, ada_b, w_in, mu_shift, w0, w2, a0, a2, k_k, k_a, g2, r_k, lnx_g, lnx_b, gmlp_norm_g, gmlp_norm_b, gmlp_ws, gmlp_bs, beta_b, conv_w, conv_b, conv_norm_g, conv_norm_b, beta_c, w_out, w_router, b_router, moe_w_gu, moe_w_down, final_g):
    batch_p, seq_p, _ = x_prompt.shape
    batch_s, seq_s, _ = x_sample.shape

    n_cond = 1 + batch_s
    cond_rows = -(-n_cond // SUBLANES) * SUBLANES
    c_all = jnp.concatenate([c_ctx[None, :], c, jnp.zeros((cond_rows - n_cond, D_MODEL), F32)], axis=0)
    mod_all = _modulation(c_all, ada_w, ada_b)

    head_id = jnp.arange(D_A) // HEAD_DIM
    head_blocks = (head_id[:, None] == head_id[None, :]).astype(BF16)
    wr_t = w_router.T
    wr_hi = wr_t.astype(BF16)
    wr_lo = (wr_t - wr_hi.astype(F32)).astype(BF16)
    b_r = b_router.reshape(N_EXPERTS, 1)
    final_g2 = final_g.reshape(1, D_MODEL)
    layer_wts = [_layer_weights(l, head_blocks, mu_shift, w0, w2, a0, a2, k_k, k_a, g2, r_k, lnx_g, lnx_b,
                                gmlp_norm_g, gmlp_norm_b, gmlp_ws, gmlp_bs, beta_b, conv_w, conv_b, conv_norm_g,
                                conv_norm_b, beta_c) for l in range(DEPTH)]
    w_in_bf = w_in.astype(BF16)
    w_out_bf = w_out.astype(BF16)
    w_gu_bf = moe_w_gu.astype(BF16).reshape(DEPTH, N_EXPERT_GROUPS, EXPERTS_PER_GROUP, D_MODEL, 2 * D_EXPERT)
    w_down_bf = moe_w_down.astype(BF16).reshape(DEPTH, N_EXPERT_GROUPS, EXPERTS_PER_GROUP * D_EXPERT, D_MODEL)
    token_id = jnp.arange(MOE_ROW_TILE)
    earlier = (token_id[:, None] < token_id[None, :]).astype(BF16)
    later = earlier.T

    groups = [
        dict(x=x_prompt.reshape(batch_p * seq_p, D_MODEL), batch=batch_p, seq=seq_p, conv_len=seq_p,
             per_batch=False, s0=None, mod_rows=slice(0, 1)),
        dict(x=x_sample.reshape(batch_s * seq_s, D_MODEL), batch=batch_s, seq=seq_s, conv_len=GRID_W,
             per_batch=True, s0=state_rwkv, mod_rows=slice(1, 1 + batch_s)),
    ]
    ctx_states = []
    outputs = []
    for grp in groups:
        x = grp["x"]
        batch, seq, per_batch = grp["batch"], grp["seq"], grp["per_batch"]
        for l in range(DEPTH):
            mod = mod_all[l, grp["mod_rows"]][:, None, :]
            wts = layer_wts[l]
            pa, pb, pc = _in_projection(x, mod, norm1_g[l].reshape(1, D_MODEL), w_in_bf[l], wts["mu_shift"], seq,
                                        per_batch)
            if grp["s0"] is None:
                s0 = jnp.zeros((batch, 2, N_HEADS_A, HEAD_DIM, HEAD_DIM), F32)
            else:
                s0 = grp["s0"][:, l]
            y_scan, s_new = _rwkv_scan(pa, s0, wts, batch, seq)
            cat = _token_mixers(pa, pb, pc, y_scan, wts, grp["conv_len"])
            x1, h2, gates_t, group_t = _out_projection(cat, x, mod, w_out_bf[l], norm2_g[l].reshape(1, D_MODEL),
                                                       wr_hi, wr_lo, b_r, seq, per_batch)
            x = _mixture_of_experts(h2, gates_t, group_t, x1, mod, w_gu_bf[l], w_down_bf[l], earlier, later,
                                    final_g2, seq, per_batch, final_norm=(l == DEPTH - 1))
            if grp["s0"] is None:
                ctx_states.append(s_new)
        outputs.append(x.reshape(batch, seq, D_MODEL))
    return (outputs[0], outputs[1], jnp.stack(ctx_states, axis=1))
```

```python
import functools
import math

import jax
import jax.numpy as jnp
from jax import lax
from jax.experimental import pallas as pl
from jax.experimental.pallas import tpu as pltpu

D_MODEL = 1024
DEPTH = 2
GRID_W = 64
HEAD_DIM = 64
D_A = 512
N_HEADS_A = 8
R_DECAY = 64
R_ICLR = 64
R_GATE = 128
D_B = 256
N_GROUPS_B = 4
GMLP_CHUNK = 128
D_C = 256
CONV_K = 31
CONV_PAD = CONV_K // 2
N_EXPERTS = 16
N_EXPERT_GROUPS = 4
EXPERTS_PER_GROUP = 4
D_EXPERT = 256
NORM_EPS = 1e-6
LN_EPS = 1e-5
LN_X_EPS = 64e-5
COLS_A = 3 * D_A + 2 * R_DECAY + 2 * R_ICLR + R_GATE
COLS_B = 2 * D_B
COLS_C = 2 * D_C
D_PROJ = COLS_A + COLS_B + COLS_C
COL_WD = 3 * D_A
COL_AD = COL_WD + 2 * R_DECAY
COL_GD = COL_AD + 2 * R_ICLR

SUBLANES = 8
LANES = 128
TIME_TILE = 256
SCAN_TILE = 256
SCAN_SEQS = 2
SCAN_CHUNK = 64
HEADS_PER_PAIR = LANES // HEAD_DIM
N_PAIRS = N_HEADS_A // HEADS_PER_PAIR
CONV_HALO = 16
ROW_TILE = 512
SHIFT_COL_CHUNK = 256
MOE_ROW_TILE = 1024
MOE_CHUNK = 128
MOD_COL_TILE = 1536
VMEM_LIMIT = 56 * 1024 * 1024

BF16 = jnp.bfloat16
F32 = jnp.float32

_NT = (((1,), (1,)), ((), ()))
_TN = (((0,), (0,)), ((), ()))


def _dot(a, b):
    return jnp.dot(a, b, preferred_element_type=F32)


def _dot_nt(a, b):
    return lax.dot_general(a, b, _NT, preferred_element_type=F32)


def _dot_tn(a, b):
    return lax.dot_general(a, b, _TN, preferred_element_type=F32)


def _split_bf16(x, parts):
    out = []
    for _ in range(parts):
        p = x.astype(BF16)
        out.append(p)
        x = x - p.astype(F32)
    return out


def _sigmoid(x):
    return 1.0 / (1.0 + jnp.exp(-x))


def _group_sum(x, bd):
    hi, lo = _split_bf16(x, 2)
    return _dot(hi, bd) + _dot(lo, bd)


def _full(shape):
    zeros = (0,) * len(shape)
    return pl.BlockSpec(shape, lambda *_: zeros)


def _params(*sem):
    return pltpu.CompilerParams(dimension_semantics=sem, vmem_limit_bytes=VMEM_LIMIT)


def _mod_kernel(c_ref, w_ref, b_ref, o_ref):
    c = c_ref[...]
    s = c * _sigmoid(c)
    o_ref[0] = _dot(s.astype(BF16), w_ref[0].astype(BF16)) + b_ref[0]


def _modulation(c_all, ada_w, ada_b):
    rows = c_all.shape[0]
    n_col = ada_w.shape[-1] // MOD_COL_TILE
    return pl.pallas_call(
        _mod_kernel,
        grid=(DEPTH, n_col),
        in_specs=[
            _full((rows, D_MODEL)),
            pl.BlockSpec((1, D_MODEL, MOD_COL_TILE), lambda l, j: (l, 0, j)),
            pl.BlockSpec((1, 1, MOD_COL_TILE), lambda l, j: (l, 0, j)),
        ],
        out_specs=pl.BlockSpec((1, rows, MOD_COL_TILE), lambda l, j: (l, 0, j)),
        out_shape=jax.ShapeDtypeStruct((DEPTH, rows, ada_w.shape[-1]), F32),
        compiler_params=_params("arbitrary", "arbitrary"),
        name="adaln_modulation",
    )(c_all, ada_w, ada_b.reshape(DEPTH, 1, -1))


def _modulated_rmsnorm(x, gain, shift, scale):
    y = x * lax.rsqrt(jnp.mean(x * x, axis=-1, keepdims=True) + NORM_EPS)
    return y * gain * (1.0 + scale) + shift


def _inproj_kernel(x_ref, xprev_ref, xnext_ref, mod_ref, g_ref, w_ref, mu_ref, pa_ref, pb_ref, pc_ref, pad_ref,
                   *, seq_len):
    tm = x_ref.shape[0]
    mod = mod_ref[0]
    norm = lambda x: _modulated_rmsnorm(x, g_ref[...], mod[:, 0:D_MODEL], mod[:, D_MODEL:2 * D_MODEL])
    h = norm(x_ref[...]).astype(BF16)
    h_halo = jnp.concatenate([norm(xprev_ref[...]), norm(xnext_ref[...])], axis=0).astype(BF16)
    pos = (pl.program_id(0) * tm + lax.broadcasted_iota(jnp.int32, (tm, 1), 0)) % seq_len
    at_start = pos == 0
    at_end = pos == seq_len - 1
    for start in range(0, COLS_A, SHIFT_COL_CHUNK):
        cols = slice(start, min(start + SHIFT_COL_CHUNK, COLS_A))
        pa = _dot(h, w_ref[:, cols])
        p_halo = _dot(h_halo, w_ref[:, cols])
        pad_ref[0:SUBLANES, cols] = p_halo[:SUBLANES]
        pad_ref[SUBLANES:SUBLANES + tm, cols] = pa
        pad_ref[SUBLANES + tm:2 * SUBLANES + tm, cols] = p_halo[SUBLANES:]
        prev = jnp.where(at_start, 0.0, pad_ref[SUBLANES - 1:SUBLANES - 1 + tm, cols])
        nxt = jnp.where(at_end, 0.0, pad_ref[SUBLANES + 1:SUBLANES + 1 + tm, cols])
        pa_ref[:, cols] = pa + mu_ref[:, cols] * (0.5 * (prev + nxt) - pa)
    pb_ref[...] = _dot(h, w_ref[:, COLS_A:COLS_A + COLS_B])
    pc_ref[...] = _dot(h, w_ref[:, COLS_A + COLS_B:])


def _mod_spec(seq_len, row_tile, per_batch):
    if per_batch:
        return pl.BlockSpec((1, 1, 6 * D_MODEL), lambda i, *_: (i * row_tile // seq_len, 0, 0))
    return pl.BlockSpec((1, 1, 6 * D_MODEL), lambda i, *_: (0, 0, 0))


def _in_projection(x, mod, norm_g, w_in, mu_shift, seq_len, per_batch):
    n = x.shape[0]
    per_tile = ROW_TILE // SUBLANES
    last_block = n // SUBLANES - 1
    row = lambda width: pl.BlockSpec((ROW_TILE, width), lambda i: (i, 0))
    prev_spec = pl.BlockSpec((SUBLANES, D_MODEL), lambda i: (jnp.maximum(i * per_tile - 1, 0), 0))
    next_spec = pl.BlockSpec((SUBLANES, D_MODEL), lambda i: (jnp.minimum((i + 1) * per_tile, last_block), 0))
    return pl.pallas_call(
        functools.partial(_inproj_kernel, seq_len=seq_len),
        grid=(n // ROW_TILE,),
        in_specs=[row(D_MODEL), prev_spec, next_spec, _mod_spec(seq_len, ROW_TILE, per_batch), _full((1, D_MODEL)),
                  _full((D_MODEL, D_PROJ)), _full((1, COLS_A))],
        out_specs=[row(COLS_A), row(COLS_B), row(COLS_C)],
        out_shape=[jax.ShapeDtypeStruct((n, COLS_A), F32), jax.ShapeDtypeStruct((n, COLS_B), F32),
                   jax.ShapeDtypeStruct((n, COLS_C), F32)],
        scratch_shapes=[pltpu.VMEM((ROW_TILE + 2 * SUBLANES, COLS_A), F32)],
        compiler_params=_params("arbitrary"),
        name="in_projection",
    )(x, x, x, mod, norm_g, w_in, mu_shift)


def _block_diag(x, same_head):
    return jnp.where(same_head, jnp.concatenate([x, x], axis=0), jnp.zeros((), x.dtype))


def _diag_blocks(x, first_head):
    return jnp.where(first_head, x[:HEAD_DIM], x[HEAD_DIM:])


def _scan_kernel(pa_ref, s0_ref, w0_ref, w2_ref, a0_ref, a2_ref, kk_ref, ka_ref, bd_ref, y_ref, sout_ref,
                 lw_s, kt_s, kn_s, b_s, state_s, g_s, sa_s, q_s, y0_s, pc_s, *, n_t):
    d = pl.program_id(0)
    t = pl.program_id(2)
    n_seq, tile = pa_ref.shape[0], pa_ref.shape[1]
    n_chunks = tile // SCAN_CHUNK
    c_len = SCAN_CHUNK
    pair_cols = lambda p: slice(p * LANES, (p + 1) * LANES)

    @pl.when(t == 0)
    def _():
        for s in range(n_seq):
            for p in range(N_PAIRS):
                state_s[s, p] = jnp.concatenate(
                    [s0_ref[s, 0, HEADS_PER_PAIR * p + j] for j in range(HEADS_PER_PAIR)], axis=1)

    for s in range(n_seq):
        rows = slice(s * tile, (s + 1) * tile)
        k = pa_ref[s, :, D_A:2 * D_A]
        wl = w0_ref[0] + _dot(jnp.tanh(pa_ref[s, :, COL_WD:COL_WD + 2 * R_DECAY]).astype(BF16), w2_ref[0])
        lw_s[rows, :] = -math.exp(-0.5) * _sigmoid(wl)
        a = _sigmoid(a0_ref[0] + _dot(pa_ref[s, :, COL_AD:COL_AD + 2 * R_ICLR].astype(BF16), a2_ref[0]))
        kx = k * kk_ref[0]
        kn = kx * lax.rsqrt(jnp.maximum(_group_sum(kx * kx, bd_ref[...]), 1e-24))
        kt_s[rows, :] = k * (1.0 + (a - 1.0) * ka_ref[0])
        kn_s[rows, :] = kn
        b_s[rows, :] = kn * a

    sign = 1 - 2 * d
    row = lax.broadcasted_iota(jnp.int32, (c_len, c_len), 0)
    col = lax.broadcasted_iota(jnp.int32, (c_len, c_len), 1)
    upto_bf = jnp.where((row - col) * sign >= 0, 1.0, 0.0).astype(BF16)
    row2 = lax.broadcasted_iota(jnp.int32, (c_len, LANES), 0)
    col2 = lax.broadcasted_iota(jnp.int32, (c_len, LANES), 1)
    tok2 = col2 % HEAD_DIM
    before = (row2 - tok2) * sign > 0
    upto = (row2 - tok2) * sign >= 0
    eye = jnp.where(row2 == tok2, 1.0, 0.0)
    first_head = col2 < HEAD_DIM
    rowb = lax.broadcasted_iota(jnp.int32, (LANES, LANES), 0)
    colb = lax.broadcasted_iota(jnp.int32, (LANES, LANES), 1)
    same_head = (rowb // HEAD_DIM) == (colb // HEAD_DIM)
    bdiag = lambda x: _block_diag(x, same_head)

    rh, kap, vv, bh, kh, bq, kq = [], [], [], [], [], [], []
    for s, c in [(s, c) for s in range(n_seq) for c in range(n_chunks)]:
        rows = slice(s * tile + c * c_len, s * tile + (c + 1) * c_len)
        in_rows = slice(c * c_len, (c + 1) * c_len)
        lwc = lw_s[rows, :]
        cum = sum(_dot(upto_bf, piece) for piece in _split_bf16(lwc, 3))
        tot = jnp.sum(lwc, axis=0, keepdims=True)
        e_neg = jnp.exp(-cum)
        e_rest = jnp.exp(tot - cum)
        kap_all = (kn_s[rows, :] * jnp.exp(cum - lwc)).astype(BF16)
        rh_all = pa_ref[s, in_rows, 0:D_A] * jnp.exp(cum)
        kt_c = kt_s[rows, :]
        b_c = b_s[rows, :]
        kh_all = (kt_c * e_neg).astype(BF16)
        bh_all = (b_c * e_neg).astype(BF16)
        kq_all = (kt_c * e_rest).astype(BF16)
        bq_all = (b_c * e_rest).astype(BF16)
        v_all = pa_ref[s, in_rows, 2 * D_A:3 * D_A].astype(BF16)
        pc_s[s * n_chunks + c] = jnp.exp(tot)
        for p in range(N_PAIRS):
            rh.append(rh_all[:, pair_cols(p)])
            kap.append(kap_all[:, pair_cols(p)])
            vv.append(v_all[:, pair_cols(p)])
            bh.append(bh_all[:, pair_cols(p)])
            kh.append(kh_all[:, pair_cols(p)])
            bq.append(bq_all[:, pair_cols(p)])
            kq.append(kq_all[:, pair_cols(p)])
    units = range(len(rh))
    both = [jnp.concatenate([kap[n], rh[n].astype(BF16)], axis=0) for n in units]
    xb = [_dot_nt(both[n], bdiag(bh[n])) for n in units]
    xk = [_dot_nt(both[n], bdiag(kh[n])) for n in units]
    l_b = [jnp.where(before, xb[n][:c_len], 0.0) for n in units]
    a_b = [jnp.where(upto, xb[n][c_len:], 0.0).astype(BF16) for n in units]
    lk_ak = [jnp.concatenate([jnp.where(before, xk[n][:c_len], 0.0), jnp.where(upto, xk[n][c_len:], 0.0)],
                             axis=0).astype(BF16) for n in units]
    lkv_akv = [_dot(lk_ak[n], bdiag(vv[n])) for n in units]
    lkv = [lkv_akv[n][:c_len].astype(BF16) for n in units]
    akv = [lkv_akv[n][c_len:] for n in units]
    inv = [eye - l_b[n] for n in units]
    lp = [l_b[n].astype(BF16) for n in units]
    lp = [_dot(lp[n], bdiag(lp[n])).astype(BF16) for n in units]
    n_sq = int(math.log2(c_len)) - 1
    for j in range(n_sq):
        if j < n_sq - 1:
            res = [_dot(jnp.concatenate([inv[n].astype(BF16), lp[n]], axis=0), bdiag(lp[n])) for n in units]
            inv = [inv[n] + res[n][:c_len] for n in units]
            lp = [res[n][c_len:].astype(BF16) for n in units]
        else:
            inv = [inv[n] + _dot(inv[n].astype(BF16), bdiag(lp[n])) for n in units]
    inv_bf = [inv[n].astype(BF16) for n in units]
    wu = [_dot(inv_bf[n], jnp.concatenate([bdiag(kap[n]), bdiag(lkv[n])], axis=1)).astype(BF16) for n in units]
    w = [wu[n][:, :LANES] for n in units]
    u = [wu[n][:, LANES:] for n in units]
    ab_wu = [_dot(a_b[n], jnp.concatenate([bdiag(w[n]), bdiag(u[n])], axis=1)) for n in units]
    g = [_diag_blocks(_dot_tn(w[n], bq[n]), first_head) for n in units]
    sa = [_diag_blocks(_dot_tn(jnp.concatenate([vv[n], -u[n]], axis=0),
                               jnp.concatenate([kq[n], bq[n]], axis=0)), first_head) for n in units]
    for n in units:
        gc, p = n // N_PAIRS, n % N_PAIRS
        q_s[gc, p] = rh[n] - ab_wu[n][:, :LANES]
        y0_s[gc, p] = akv[n] - ab_wu[n][:, LANES:]
        g_s[gc, p] = g[n]
        sa_s[gc, p] = sa[n]

    chains = [(s, p) for s in range(n_seq) for p in range(N_PAIRS)]

    def advance(i, carry):
        c = i + d * (n_chunks - 1 - 2 * i)
        rows = pl.ds(pl.multiple_of(c * c_len, c_len), c_len)
        st = [state_s[s, p] for s, p in chains]
        st_bf = [x.astype(BF16) for x in st]
        sg = [_dot(st_bf[n], bdiag(g_s[s * n_chunks + c, p].astype(BF16))) for n, (s, p) in enumerate(chains)]
        qs = [_dot_nt(q_s[s * n_chunks + c, p].astype(BF16), bdiag(st_bf[n])) for n, (s, p) in enumerate(chains)]
        for n, (s, p) in enumerate(chains):
            gc = s * n_chunks + c
            state_s[s, p] = st[n] * pc_s[gc][:, pair_cols(p)] - sg[n] + sa_s[gc, p]
            y_ref[0, s, rows, pair_cols(p)] = qs[n] + y0_s[gc, p]
        return carry

    lax.fori_loop(0, n_chunks, advance, 0, unroll=True)

    @pl.when(t == n_t - 1)
    def _():
        for s, p in chains:
            st = state_s[s, p]
            for j in range(HEADS_PER_PAIR):
                sout_ref[s, 0, HEADS_PER_PAIR * p + j] = st[:, j * HEAD_DIM:(j + 1) * HEAD_DIM]


def _rwkv_scan(pa, s0, wts, batch, seq_len):
    tile = min(SCAN_TILE, seq_len)
    n_t = seq_len // tile
    n_chunks = SCAN_SEQS * tile // SCAN_CHUNK

    def time_tile(d, t):
        return t + d * (n_t - 1 - 2 * t)

    per_dir = lambda shape: pl.BlockSpec((1,) + shape, lambda d, b, t: (d,) + (0,) * len(shape))
    state_spec = pl.BlockSpec((SCAN_SEQS, 1, N_HEADS_A, HEAD_DIM, HEAD_DIM), lambda d, b, t: (b, d, 0, 0, 0))
    pair_mat = pltpu.VMEM((n_chunks, N_PAIRS, SCAN_CHUNK, LANES), F32)
    stream = pltpu.VMEM((SCAN_SEQS * tile, D_A), F32)
    y, s_new = pl.pallas_call(
        functools.partial(_scan_kernel, n_t=n_t),
        grid=(2, batch // SCAN_SEQS, n_t),
        in_specs=[
            pl.BlockSpec((SCAN_SEQS, tile, COLS_A), lambda d, b, t: (b, time_tile(d, t), 0)), state_spec,
            per_dir((1, D_A)), per_dir((2 * R_DECAY, D_A)), per_dir((1, D_A)), per_dir((2 * R_ICLR, D_A)),
            per_dir((1, D_A)), per_dir((1, D_A)),
            _full((D_A, D_A)),
        ],
        out_specs=[pl.BlockSpec((1, SCAN_SEQS, tile, D_A), lambda d, b, t: (d, b, time_tile(d, t), 0)), state_spec],
        out_shape=[jax.ShapeDtypeStruct((2, batch, seq_len, D_A), F32),
                   jax.ShapeDtypeStruct((batch, 2, N_HEADS_A, HEAD_DIM, HEAD_DIM), F32)],
        scratch_shapes=[
            stream, stream, stream, stream,
            pltpu.VMEM((SCAN_SEQS, N_PAIRS, HEAD_DIM, LANES), F32),
            pair_mat, pair_mat, pair_mat, pair_mat,
            pltpu.VMEM((n_chunks, 1, D_A), F32),
        ],
        compiler_params=_params("arbitrary", "arbitrary", "arbitrary"),
        name="rwkv7_scan",
    )(pa.reshape(batch, seq_len, COLS_A), s0, wts["w0"], wts["w2_pad"], wts["a0"], wts["a2_pad"], wts["k_k"],
      wts["k_a"], wts["head_blocks"])
    return y.reshape(2, batch * seq_len, D_A), s_new


def _layernorm(x, g, b):
    mu = jnp.mean(x, axis=-1, keepdims=True)
    xc = x - mu
    var = jnp.mean(xc * xc, axis=-1, keepdims=True)
    return xc * lax.rsqrt(var + LN_EPS) * g + b


def _rmsnorm(x, g):
    return x * lax.rsqrt(jnp.mean(x * x, axis=-1, keepdims=True) + NORM_EPS) * g


def _mixer_kernel(pa_ref, pb_ref, pc_ref, yf_ref, yb_ref,
                  a0_ref, a2_ref, ka_ref, g2_ref, rk_ref, lnxg_ref, lnxb_ref, bd_ref,
                  gng_ref, gnb_ref, ws_ref, bs_ref, betab_ref,
                  cw_ref, cb_ref, cng_ref, cnb_ref, betac_ref,
                  cat_ref, cpad_ref, cphase_ref, *, conv_len):
    bd = bd_ref[...]

    r = pa_ref[:, 0:D_A]
    k = pa_ref[:, D_A:2 * D_A]
    v = pa_ref[:, 2 * D_A:3 * D_A]
    ad = pa_ref[:, COL_AD:COL_AD + 2 * R_ICLR].astype(BF16)
    gd = pa_ref[:, COL_GD:COL_GD + R_GATE]
    a_f = _sigmoid(a0_ref[0] + _dot(ad, a2_ref[0]))
    a_b = _sigmoid(a0_ref[1] + _dot(ad, a2_ref[1]))
    kt_sum = k * (2.0 + (a_f - 1.0) * ka_ref[0] + (a_b - 1.0) * ka_ref[1])
    bonus = _group_sum(r * kt_sum * rk_ref[...], bd) * v
    gate = _dot(_sigmoid(gd).astype(BF16), g2_ref[...])
    y = yf_ref[0] + yb_ref[0]
    mean = _group_sum(y, bd) * (1.0 / HEAD_DIM)
    yc = y - mean
    var = _group_sum(yc * yc, bd) * (1.0 / HEAD_DIM)
    yn = yc * lax.rsqrt(var + LN_X_EPS) * lnxg_ref[...] + lnxb_ref[...]
    cat_ref[:, 0:D_A] = ((yn + bonus) * gate).astype(BF16)

    gb = jax.nn.gelu(pb_ref[...])
    u = gb[:, :D_B]
    vg = _layernorm(gb[:, D_B:], gng_ref[...], gnb_ref[...]).astype(BF16)
    head_of_lane = lax.broadcasted_iota(jnp.int32, (GMLP_CHUNK, D_B), 1) // HEAD_DIM
    sv_chunks = []
    for n in range(TIME_TILE // GMLP_CHUNK):
        vgc = vg[n * GMLP_CHUNK:(n + 1) * GMLP_CHUNK]
        sv = bs_ref[...]
        for g in range(N_GROUPS_B):
            sv = sv + jnp.where(head_of_lane == g, _dot(ws_ref[g], vgc), 0.0)
        sv_chunks.append(sv)
    sv = jnp.concatenate(sv_chunks, axis=0)
    cat_ref[:, D_A:D_A + D_B] = _rmsnorm(u * sv, betab_ref[...]).astype(BF16)

    pc = pc_ref[...]
    gl = pc[:, :D_C] * _sigmoid(pc[:, D_C:])
    stride = conv_len + 2 * CONV_HALO
    zeros = jnp.zeros((CONV_HALO, D_C), F32)
    convs = []
    for q in range(TIME_TILE // conv_len):
        base = q * stride
        cpad_ref[base:base + CONV_HALO, :] = zeros
        cpad_ref[base + CONV_HALO:base + CONV_HALO + conv_len, :] = gl[q * conv_len:(q + 1) * conv_len]
        cpad_ref[base + CONV_HALO + conv_len:base + stride, :] = zeros
    total = (TIME_TILE // conv_len) * stride
    for s in range(1, SUBLANES):
        cphase_ref[s, 0:total - SUBLANES, :] = cpad_ref[s:s + total - SUBLANES, :]
    for q in range(TIME_TILE // conv_len):
        base = q * stride + CONV_HALO - CONV_PAD
        acc = jnp.zeros((conv_len, D_C), F32) + cb_ref[...]
        for j in range(CONV_K):
            phase = (base + j) % SUBLANES
            first = base + j - phase
            if phase == 0:
                taps = cpad_ref[first:first + conv_len, :]
            else:
                taps = cphase_ref[phase, first:first + conv_len, :]
            acc = acc + cw_ref[j:j + 1, :] * taps
        convs.append(acc)
    conv = jnp.concatenate(convs, axis=0) if len(convs) > 1 else convs[0]
    z = _layernorm(conv, cng_ref[...], cnb_ref[...])
    z = z * _sigmoid(z)
    cat_ref[:, D_A + D_B:] = _rmsnorm(z, betac_ref[...]).astype(BF16)


def _token_mixers(pa, pb, pc, y_scan, wts, conv_len):
    n = pa.shape[0]
    tile = lambda width: pl.BlockSpec((TIME_TILE, width), lambda i: (i, 0))
    y_dir = lambda d: pl.BlockSpec((1, TIME_TILE, D_A), lambda i: (d, i, 0))
    n_conv = TIME_TILE // conv_len
    return pl.pallas_call(
        functools.partial(_mixer_kernel, conv_len=conv_len),
        grid=(n // TIME_TILE,),
        in_specs=[
            tile(COLS_A), tile(COLS_B), tile(COLS_C), y_dir(0), y_dir(1),
            _full((2, 1, D_A)), _full((2, 2 * R_ICLR, D_A)), _full((2, 1, D_A)),
            _full((R_GATE, D_A)), _full((1, D_A)), _full((1, D_A)), _full((1, D_A)), _full((D_A, D_A)),
            _full((1, D_B)), _full((1, D_B)), _full((N_GROUPS_B, GMLP_CHUNK, GMLP_CHUNK)),
            _full((GMLP_CHUNK, D_B)), _full((1, D_B)),
            _full((CONV_K, D_C)), _full((1, D_C)), _full((1, D_C)), _full((1, D_C)), _full((1, D_C)),
        ],
        out_specs=tile(D_MODEL),
        out_shape=jax.ShapeDtypeStruct((n, D_MODEL), BF16),
        scratch_shapes=[pltpu.VMEM((n_conv * (conv_len + 2 * CONV_HALO), D_C), F32),
                        pltpu.VMEM((SUBLANES, n_conv * (conv_len + 2 * CONV_HALO), D_C), F32)],
        compiler_params=_params("arbitrary"),
        name="token_mixers",
    )(pa, pb, pc, y_scan, y_scan,
      wts["a0"], wts["a2_pad"], wts["k_a"], wts["g2"], wts["r_k"], wts["lnx_g"], wts["lnx_b"],
      wts["head_blocks"], wts["gmlp_norm_g"], wts["gmlp_norm_b"], wts["gmlp_ws"], wts["gmlp_bias"],
      wts["beta_b"], wts["conv_w"], wts["conv_b"], wts["conv_norm_g"], wts["conv_norm_b"], wts["beta_c"])


def _top2_sum(a, b, c, d):
    hi1, lo1 = jnp.maximum(a, b), jnp.minimum(a, b)
    hi2, lo2 = jnp.maximum(c, d), jnp.minimum(c, d)
    return jnp.maximum(hi1, hi2) + jnp.maximum(jnp.minimum(hi1, hi2), jnp.maximum(lo1, lo2))


def _router_gates(logits_t, bias_ref):
    m = jnp.max(logits_t, axis=0, keepdims=True)
    e = jnp.exp(logits_t - m)
    probs = e / jnp.sum(e, axis=0, keepdims=True)
    sel = probs + bias_ref[...]
    p_row = [probs[i:i + 1, :] for i in range(N_EXPERTS)]
    s_row = [sel[i:i + 1, :] for i in range(N_EXPERTS)]
    best_val = None
    best = None
    for g in range(N_EXPERT_GROUPS):
        score = _top2_sum(*s_row[g * EXPERTS_PER_GROUP:(g + 1) * EXPERTS_PER_GROUP])
        if g == 0:
            best_val, best = score, jnp.zeros_like(score, dtype=jnp.int32)
        else:
            better = score > best_val
            best_val = jnp.where(better, score, best_val)
            best = jnp.where(better, g, best)
    chosen = []
    for i in range(N_EXPERTS):
        g = i // EXPERTS_PER_GROUP
        rank = jnp.zeros_like(best)
        for j in range(g * EXPERTS_PER_GROUP, (g + 1) * EXPERTS_PER_GROUP):
            if j == i:
                continue
            ahead = (s_row[j] >= s_row[i]) if j < i else (s_row[j] > s_row[i])
            rank = rank + ahead.astype(jnp.int32)
        chosen.append((best == g) & (rank < 2))
    picked = [jnp.where(chosen[i], p_row[i], 0.0) for i in range(N_EXPERTS)]
    denom = picked[0]
    for i in range(1, N_EXPERTS):
        denom = denom + picked[i]
    gates = jnp.concatenate([p / denom for p in picked], axis=0)
    group_rows = [jnp.where(best == g, 1.0, 0.0) for g in range(N_EXPERT_GROUPS)]
    group_rows += [jnp.zeros_like(group_rows[0])] * (SUBLANES - N_EXPERT_GROUPS)
    return gates, jnp.concatenate(group_rows, axis=0)


def _outproj_kernel(cat_ref, x_ref, mod_ref, wout_ref, g_ref, wr_hi_ref, wr_lo_ref, br_ref,
                    x1_ref, h2_ref, gates_ref, group_ref):
    mod = mod_ref[0]
    out = _dot(cat_ref[...], wout_ref[...])
    x1 = x_ref[...] + mod[:, 2 * D_MODEL:3 * D_MODEL] * out
    x1_ref[...] = x1
    h2 = _modulated_rmsnorm(x1, g_ref[...], mod[:, 3 * D_MODEL:4 * D_MODEL], mod[:, 4 * D_MODEL:5 * D_MODEL])
    h_hi, h_lo = _split_bf16(h2, 2)
    h2_ref[...] = h_hi
    logits_t = (_dot_nt(wr_hi_ref[...], h_hi) + _dot_nt(wr_hi_ref[...], h_lo) + _dot_nt(wr_lo_ref[...], h_hi))
    gates_ref[...], group_ref[...] = _router_gates(logits_t, br_ref)


def _out_projection(cat, x, mod, w_out, norm_g, wr_hi, wr_lo, b_router, seq_len, per_batch):
    n = x.shape[0]
    row = lambda width: pl.BlockSpec((ROW_TILE, width), lambda i: (i, 0))
    return pl.pallas_call(
        _outproj_kernel,
        grid=(n // ROW_TILE,),
        in_specs=[row(D_MODEL), row(D_MODEL), _mod_spec(seq_len, ROW_TILE, per_batch),
                  _full((D_MODEL, D_MODEL)), _full((1, D_MODEL)),
                  _full((N_EXPERTS, D_MODEL)), _full((N_EXPERTS, D_MODEL)), _full((N_EXPERTS, 1))],
        out_specs=[row(D_MODEL), row(D_MODEL), pl.BlockSpec((N_EXPERTS, ROW_TILE), lambda i: (0, i)),
                   pl.BlockSpec((SUBLANES, ROW_TILE), lambda i: (0, i))],
        out_shape=[jax.ShapeDtypeStruct((n, D_MODEL), F32), jax.ShapeDtypeStruct((n, D_MODEL), BF16),
                   jax.ShapeDtypeStruct((N_EXPERTS, n), F32), jax.ShapeDtypeStruct((SUBLANES, n), F32)],
        compiler_params=_params("arbitrary"),
        name="out_projection_router",
    )(cat, x, mod, w_out, norm_g, wr_hi, wr_lo, b_router)


def _moe_kernel(start_ref, h_ref, gates_ref, grp_t_ref, grp_c_ref, earlier_ref, later_ref, x1_ref, mod_ref, wgu_ref,
                wdn_ref, fg_ref, o_ref, xs_ref, gs_ref, ys_ref, pt_ref, *, final_norm):
    blk = pl.program_id(0)
    grp = pl.program_id(1)
    tm = h_ref.shape[0]

    @pl.when(grp == 0)
    def _():
        earlier = earlier_ref[...]
        grp_t = grp_t_ref[...]
        grp_c = grp_c_ref[...]
        rank_t = _dot(grp_t.astype(BF16), earlier)
        rank_c = _dot(later_ref[...], grp_c.astype(BF16))
        slot_t = jnp.zeros((1, tm), F32)
        slot_c = jnp.zeros((tm, 1), F32)
        for g in range(N_EXPERT_GROUPS):
            first = start_ref[blk, g].astype(F32)
            slot_t = slot_t + grp_t[g:g + 1, :] * (rank_t[g:g + 1, :] + first)
            slot_c = slot_c + grp_c[:, g:g + 1] * (rank_c[:, g:g + 1] + first)
        slot_rows = lax.broadcasted_iota(jnp.int32, (tm, tm), 0)
        slot_lanes = lax.broadcasted_iota(jnp.int32, (tm, tm), 1)
        to_slots = jnp.where(slot_rows == slot_t.astype(jnp.int32), 1.0, 0.0).astype(BF16)
        pt_ref[...] = jnp.where(slot_lanes == slot_c.astype(jnp.int32), 1.0, 0.0).astype(BF16)
        xs_ref[...] = _dot(to_slots, h_ref[...]).astype(BF16)
        pieces = _dot(to_slots, jnp.concatenate(_split_bf16(gates_ref[...], 3), axis=1))
        gs_ref[...] = (pieces[:, 0:N_EXPERTS] + pieces[:, N_EXPERTS:2 * N_EXPERTS]
                       + pieces[:, 2 * N_EXPERTS:3 * N_EXPERTS])
        ys_ref[...] = jnp.zeros_like(ys_ref)

    lo = start_ref[blk, grp] // MOE_CHUNK
    hi = (start_ref[blk, grp + 1] + MOE_CHUNK - 1) // MOE_CHUNK
    lane = lax.broadcasted_iota(jnp.int32, (MOE_CHUNK, N_EXPERTS), 1)

    def chunk(j, carry):
        rows = pl.ds(pl.multiple_of(j * MOE_CHUNK, MOE_CHUNK), MOE_CHUNK)
        x = xs_ref[rows, :]
        gates = gs_ref[rows, :]
        acts = []
        for e in range(EXPERTS_PER_GROUP):
            gate = jnp.sum(jnp.where(lane == grp * EXPERTS_PER_GROUP + e, gates, 0.0), axis=1, keepdims=True)
            gu = _dot(x, wgu_ref[0, e])
            g = gu[:, :D_EXPERT]
            acts.append((g * _sigmoid(g) * gu[:, D_EXPERT:] * gate).astype(BF16))
        ys_ref[rows, :] += _dot(jnp.concatenate(acts, axis=1), wdn_ref[0])
        return carry

    lax.fori_loop(lo, hi, chunk, 0)

    @pl.when(grp == N_EXPERT_GROUPS - 1)
    def _():
        moe = _dot(pt_ref[...], ys_ref[...].astype(BF16))
        x2 = x1_ref[...] + mod_ref[0][:, 5 * D_MODEL:6 * D_MODEL] * moe
        if final_norm:
            x2 = _rmsnorm(x2, fg_ref[...])
        o_ref[...] = x2


def _mixture_of_experts(h2, gates_t, group_t, x1, mod, w_gu, w_down, earlier, later, final_g, seq_len, per_batch,
                        final_norm):
    n = x1.shape[0]
    n_blocks = n // MOE_ROW_TILE
    counts = group_t[:N_EXPERT_GROUPS].reshape(N_EXPERT_GROUPS, n_blocks, MOE_ROW_TILE).sum(axis=-1)
    starts = jnp.concatenate([jnp.zeros((1, n_blocks), F32), jnp.cumsum(counts, axis=0)], axis=0)
    starts = starts.T.astype(jnp.int32)
    row = lambda width: pl.BlockSpec((MOE_ROW_TILE, width), lambda i, g, s: (i, 0))
    col = lambda height: pl.BlockSpec((height, MOE_ROW_TILE), lambda i, g, s: (0, i))
    order = pl.BlockSpec((MOE_ROW_TILE, MOE_ROW_TILE), lambda i, g, s: (0, 0))
    mod_spec = _mod_spec(seq_len, MOE_ROW_TILE, per_batch)
    grid_spec = pltpu.PrefetchScalarGridSpec(
        num_scalar_prefetch=1,
        grid=(n_blocks, N_EXPERT_GROUPS),
        in_specs=[row(D_MODEL), row(N_EXPERTS), col(SUBLANES), row(SUBLANES), order, order,
                  row(D_MODEL), mod_spec,
                  pl.BlockSpec((1, EXPERTS_PER_GROUP, D_MODEL, 2 * D_EXPERT), lambda i, g, s: (g, 0, 0, 0)),
                  pl.BlockSpec((1, EXPERTS_PER_GROUP * D_EXPERT, D_MODEL), lambda i, g, s: (g, 0, 0)),
                  pl.BlockSpec((1, D_MODEL), lambda i, g, s: (0, 0))],
        out_specs=row(D_MODEL),
        scratch_shapes=[pltpu.VMEM((MOE_ROW_TILE, D_MODEL), BF16), pltpu.VMEM((MOE_ROW_TILE, N_EXPERTS), F32),
                        pltpu.VMEM((MOE_ROW_TILE, D_MODEL), F32), pltpu.VMEM((MOE_ROW_TILE, MOE_ROW_TILE), BF16)],
    )
    return pl.pallas_call(
        functools.partial(_moe_kernel, final_norm=final_norm),
        grid_spec=grid_spec,
        out_shape=jax.ShapeDtypeStruct((n, D_MODEL), F32),
        compiler_params=_params("arbitrary", "arbitrary"),
        name="mixture_of_experts",
    )(starts, h2, gates_t.T, group_t, group_t.T, earlier, later, x1, mod, w_gu, w_down, final_g)


def _pad_low_rank(w, rank):
    z = jnp.zeros_like(w[0])
    return jnp.stack([jnp.concatenate([w[0], z], axis=0), jnp.concatenate([z, w[1]], axis=0)]).astype(BF16)


def _layer_weights(l, head_blocks, mu_shift, w0, w2, a0, a2, k_k, k_a, g2, r_k, lnx_g, lnx_b, gmlp_norm_g,
                   gmlp_norm_b, gmlp_ws, gmlp_bs, beta_b, conv_w, conv_b, conv_norm_g, conv_norm_b, beta_c):
    return dict(
        mu_shift=mu_shift[l].reshape(1, COLS_A),
        w0=w0[l].reshape(2, 1, D_A), w2_pad=_pad_low_rank(w2[l], R_DECAY),
        a0=a0[l].reshape(2, 1, D_A), a2_pad=_pad_low_rank(a2[l], R_ICLR),
        k_k=k_k[l].reshape(2, 1, D_A), k_a=k_a[l].reshape(2, 1, D_A),
        g2=g2[l].astype(BF16), r_k=r_k[l].reshape(1, D_A),
        lnx_g=lnx_g[l].reshape(1, D_A), lnx_b=lnx_b[l].reshape(1, D_A), head_blocks=head_blocks,
        gmlp_norm_g=gmlp_norm_g[l].reshape(1, D_B), gmlp_norm_b=gmlp_norm_b[l].reshape(1, D_B),
        gmlp_ws=gmlp_ws[l].astype(BF16), gmlp_bias=jnp.repeat(gmlp_bs[l].T, HEAD_DIM, axis=1),
        beta_b=beta_b[l].reshape(1, D_B),
        conv_w=conv_w[l], conv_b=conv_b[l].reshape(1, D_C),
        conv_norm_g=conv_norm_g[l].reshape(1, D_C), conv_norm_b=conv_norm_b[l].reshape(1, D_C),
        beta_c=beta_c[l].reshape(1, D_C),
    )


def kernel(x_prompt, x_sample, state_rwkv, c, c_ctx, norm1_g, norm2_g, ada_w, ada_b, w_in, mu_shift, w0, w2, a0, a2, k_k, k_a, g2, r_k, lnx_g, lnx_b, gmlp_norm_g, gmlp_norm_b, gmlp_ws, gmlp_bs, beta_b, conv_w, conv_b, conv_norm_g, conv_norm_b, beta_c, w_out, w_router, b_router, moe_w_gu, moe_w_down, final_g):
    batch_p, seq_p, _ = x_prompt.shape
    batch_s, seq_s, _ = x_sample.shape

    n_cond = 1 + batch_s
    cond_rows = -(-n_cond // SUBLANES) * SUBLANES
    c_all = jnp.concatenate([c_ctx[None, :], c, jnp.zeros((cond_rows - n_cond, D_MODEL), F32)], axis=0)
    mod_all = _modulation(c_all, ada_w, ada_b)

    head_id = jnp.arange(D_A) // HEAD_DIM
    head_blocks = (head_id[:, None] == head_id[None, :]).astype(BF16)
    wr_t = w_router.T
    wr_hi = wr_t.astype(BF16)
    wr_lo = (wr_t - wr_hi.astype(F32)).astype(BF16)
    b_r = b_router.reshape(N_EXPERTS, 1)
    final_g2 = final_g.reshape(1, D_MODEL)
    layer_wts = [_layer_weights(l, head_blocks, mu_shift, w0, w2, a0, a2, k_k, k_a, g2, r_k, lnx_g, lnx_b,
                                gmlp_norm_g, gmlp_norm_b, gmlp_ws, gmlp_bs, beta_b, conv_w, conv_b, conv_norm_g,
                                conv_norm_b, beta_c) for l in range(DEPTH)]
    w_in_bf = [w_in[l].astype(BF16) for l in range(DEPTH)]
    w_out_bf = [w_out[l].astype(BF16) for l in range(DEPTH)]
    w_gu_bf = [moe_w_gu[l].astype(BF16).reshape(N_EXPERT_GROUPS, EXPERTS_PER_GROUP, D_MODEL, 2 * D_EXPERT)
               for l in range(DEPTH)]
    w_down_bf = [moe_w_down[l].astype(BF16).reshape(N_EXPERT_GROUPS, EXPERTS_PER_GROUP * D_EXPERT, D_MODEL)
                 for l in range(DEPTH)]
    token_id = jnp.arange(MOE_ROW_TILE)
    earlier = (token_id[:, None] < token_id[None, :]).astype(BF16)
    later = earlier.T

    groups = [
        dict(x=x_prompt.reshape(batch_p * seq_p, D_MODEL), batch=batch_p, seq=seq_p, conv_len=seq_p,
             per_batch=False, s0=None, mod_rows=slice(0, 1)),
        dict(x=x_sample.reshape(batch_s * seq_s, D_MODEL), batch=batch_s, seq=seq_s, conv_len=GRID_W,
             per_batch=True, s0=state_rwkv, mod_rows=slice(1, 1 + batch_s)),
    ]
    ctx_states = []
    outputs = []
    for grp in groups:
        x = grp["x"]
        batch, seq, per_batch = grp["batch"], grp["seq"], grp["per_batch"]
        for l in range(DEPTH):
            mod = mod_all[l, grp["mod_rows"]][:, None, :]
            wts = layer_wts[l]
            pa, pb, pc = _in_projection(x, mod, norm1_g[l].reshape(1, D_MODEL), w_in_bf[l], wts["mu_shift"], seq,
                                        per_batch)
            if grp["s0"] is None:
                s0 = jnp.zeros((batch, 2, N_HEADS_A, HEAD_DIM, HEAD_DIM), F32)
            else:
                s0 = grp["s0"][:, l]
            y_scan, s_new = _rwkv_scan(pa, s0, wts, batch, seq)
            cat = _token_mixers(pa, pb, pc, y_scan, wts, grp["conv_len"])
            x1, h2, gates_t, group_t = _out_projection(cat, x, mod, w_out_bf[l], norm2_g[l].reshape(1, D_MODEL),
                                                       wr_hi, wr_lo, b_r, seq, per_batch)
            x = _mixture_of_experts(h2, gates_t, group_t, x1, mod, w_gu_bf[l], w_down_bf[l], earlier, later,
                                    final_g2, seq, per_batch, final_norm=(l == DEPTH - 1))
            if grp["s0"] is None:
                ctx_states.append(s_new)
        outputs.append(x.reshape(batch, seq, D_MODEL))
    return (outputs[0], outputs[1], jnp.stack(ctx_states, axis=1))
```

```python
import functools
import math

import jax
import jax.numpy as jnp
from jax import lax
from jax.experimental import pallas as pl
from jax.experimental.pallas import tpu as pltpu

D_MODEL = 1024
DEPTH = 2
GRID_W = 64
HEAD_DIM = 64
D_A = 512
N_HEADS_A = 8
R_DECAY = 64
R_ICLR = 64
R_GATE = 128
D_B = 256
N_GROUPS_B = 4
GMLP_CHUNK = 128
D_C = 256
CONV_K = 31
CONV_PAD = CONV_K // 2
N_EXPERTS = 16
N_EXPERT_GROUPS = 4
EXPERTS_PER_GROUP = 4
D_EXPERT = 256
NORM_EPS = 1e-6
LN_EPS = 1e-5
LN_X_EPS = 64e-5
COLS_A = 3 * D_A + 2 * R_DECAY + 2 * R_ICLR + R_GATE
COLS_B = 2 * D_B
COLS_C = 2 * D_C
D_PROJ = COLS_A + COLS_B + COLS_C
COL_WD = 3 * D_A
COL_AD = COL_WD + 2 * R_DECAY
COL_GD = COL_AD + 2 * R_ICLR

SUBLANES = 8
GROUP_LANES = 128
TIME_TILE = 256
SCAN_TILE = 256
SCAN_SEQS = 2
SCAN_CHUNK = 64
HEADS_PER_PAIR = GROUP_LANES // HEAD_DIM
N_PAIRS = N_HEADS_A // HEADS_PER_PAIR
CONV_HALO = 16
ROW_TILE = 512
INPROJ_ROW_TILE = 512
SHIFT_COL_CHUNK = 256
MOE_ROW_TILE = 1024
MOE_CHUNK = 128
MOD_COL_TILE = 1536
VMEM_LIMIT = 56 * 1024 * 1024

BF16 = jnp.bfloat16
F32 = jnp.float32

_NT = (((1,), (1,)), ((), ()))
_TN = (((0,), (0,)), ((), ()))


def _dot(a, b):
    return jnp.dot(a, b, preferred_element_type=F32)


def _dot_nt(a, b):
    return lax.dot_general(a, b, _NT, preferred_element_type=F32)


def _dot_tn(a, b):
    return lax.dot_general(a, b, _TN, preferred_element_type=F32)


def _split_bf16(x, parts):
    out = []
    for _ in range(parts):
        p = x.astype(BF16)
        out.append(p)
        x = x - p.astype(F32)
    return out


def _sigmoid(x):
    return 1.0 / (1.0 + jnp.exp(-x))


def _group_sum(x, bd):
    hi, lo = _split_bf16(x, 2)
    return _dot(hi, bd) + _dot(lo, bd)


def _full(shape):
    zeros = (0,) * len(shape)
    return pl.BlockSpec(shape, lambda *_: zeros)


def _params(*sem):
    return pltpu.CompilerParams(dimension_semantics=sem, vmem_limit_bytes=VMEM_LIMIT)


def _mod_kernel(c_ref, w_ref, b_ref, o_ref):
    c = c_ref[...]
    s = c * _sigmoid(c)
    o_ref[0] = _dot(s.astype(BF16), w_ref[0].astype(BF16)) + b_ref[0]


def _modulation(c_all, ada_w, ada_b):
    rows = c_all.shape[0]
    n_col = ada_w.shape[-1] // MOD_COL_TILE
    return pl.pallas_call(
        _mod_kernel,
        grid=(DEPTH, n_col),
        in_specs=[
            _full((rows, D_MODEL)),
            pl.BlockSpec((1, D_MODEL, MOD_COL_TILE), lambda l, j: (l, 0, j)),
            pl.BlockSpec((1, 1, MOD_COL_TILE), lambda l, j: (l, 0, j)),
        ],
        out_specs=pl.BlockSpec((1, rows, MOD_COL_TILE), lambda l, j: (l, 0, j)),
        out_shape=jax.ShapeDtypeStruct((DEPTH, rows, ada_w.shape[-1]), F32),
        compiler_params=_params("arbitrary", "arbitrary"),
        name="adaln_modulation",
    )(c_all, ada_w, ada_b.reshape(DEPTH, 1, -1))


def _modulated_rmsnorm(x, gain, shift, scale):
    y = x * lax.rsqrt(jnp.mean(x * x, axis=-1, keepdims=True) + NORM_EPS)
    return y * gain * (1.0 + scale) + shift


def _inproj_kernel(x_ref, xprev_ref, xnext_ref, mod_ref, g_ref, w_ref, mu_ref, pa_ref, pb_ref, pc_ref, pad_ref,
                   *, seq_len):
    tm = x_ref.shape[0]
    mod = mod_ref[0]
    norm = lambda x: _modulated_rmsnorm(x, g_ref[...], mod[:, 0:D_MODEL], mod[:, D_MODEL:2 * D_MODEL])
    h = norm(x_ref[...]).astype(BF16)
    h_halo = jnp.concatenate([norm(xprev_ref[...]), norm(xnext_ref[...])], axis=0).astype(BF16)
    pos = (pl.program_id(0) * tm + lax.broadcasted_iota(jnp.int32, (tm, 1), 0)) % seq_len
    at_start = pos == 0
    at_end = pos == seq_len - 1
    for start in range(0, COLS_A, SHIFT_COL_CHUNK):
        cols = slice(start, min(start + SHIFT_COL_CHUNK, COLS_A))
        pa = _dot(h, w_ref[:, cols])
        p_halo = _dot(h_halo, w_ref[:, cols])
        pad_ref[0:SUBLANES, cols] = p_halo[:SUBLANES]
        pad_ref[SUBLANES:SUBLANES + tm, cols] = pa
        pad_ref[SUBLANES + tm:2 * SUBLANES + tm, cols] = p_halo[SUBLANES:]
        prev = jnp.where(at_start, 0.0, pad_ref[SUBLANES - 1:SUBLANES - 1 + tm, cols])
        nxt = jnp.where(at_end, 0.0, pad_ref[SUBLANES + 1:SUBLANES + 1 + tm, cols])
        pa_ref[:, cols] = pa + mu_ref[:, cols] * (0.5 * (prev + nxt) - pa)
    pb_ref[...] = _dot(h, w_ref[:, COLS_A:COLS_A + COLS_B])
    pc_ref[...] = _dot(h, w_ref[:, COLS_A + COLS_B:])


def _mod_spec(seq_len, row_tile, per_batch):
    if per_batch:
        return pl.BlockSpec((1, 1, 6 * D_MODEL), lambda i, *_: (i * row_tile // seq_len, 0, 0))
    return pl.BlockSpec((1, 1, 6 * D_MODEL), lambda i, *_: (0, 0, 0))


def _in_projection(x, mod, norm_g, w_in, mu_shift, seq_len, per_batch):
    n = x.shape[0]
    tm = INPROJ_ROW_TILE
    per_tile = tm // SUBLANES
    last_block = n // SUBLANES - 1
    row = lambda width: pl.BlockSpec((tm, width), lambda i: (i, 0))
    prev_spec = pl.BlockSpec((SUBLANES, D_MODEL), lambda i: (jnp.maximum(i * per_tile - 1, 0), 0))
    next_spec = pl.BlockSpec((SUBLANES, D_MODEL), lambda i: (jnp.minimum((i + 1) * per_tile, last_block), 0))
    return pl.pallas_call(
        functools.partial(_inproj_kernel, seq_len=seq_len),
        grid=(n // tm,),
        in_specs=[row(D_MODEL), prev_spec, next_spec, _mod_spec(seq_len, tm, per_batch), _full((1, D_MODEL)),
                  _full((D_MODEL, D_PROJ)), _full((1, COLS_A))],
        out_specs=[row(COLS_A), row(COLS_B), row(COLS_C)],
        out_shape=[jax.ShapeDtypeStruct((n, COLS_A), F32), jax.ShapeDtypeStruct((n, COLS_B), F32),
                   jax.ShapeDtypeStruct((n, COLS_C), F32)],
        scratch_shapes=[pltpu.VMEM((tm + 2 * SUBLANES, COLS_A), F32)],
        compiler_params=_params("arbitrary"),
        name="in_projection",
    )(x, x, x, mod, norm_g, w_in, mu_shift)


def _block_diag(x, same_head):
    return jnp.where(same_head, jnp.concatenate([x] * HEADS_PER_PAIR, axis=0), jnp.zeros((), x.dtype))


def _diag_blocks(x, head_of_lane):
    out = x[:HEAD_DIM]
    for j in range(1, HEADS_PER_PAIR):
        out = jnp.where(head_of_lane == j, x[j * HEAD_DIM:(j + 1) * HEAD_DIM], out)
    return out


def _scan_kernel(pa_ref, s0_ref, w0_ref, w2_ref, a0_ref, a2_ref, kk_ref, ka_ref, bd_ref, y_ref, sout_ref,
                 lw_s, kt_s, kn_s, b_s, state_s, g_s, sa_s, q_s, y0_s, pc_s, *, n_t):
    d = pl.program_id(0)
    t = pl.program_id(2)
    n_seq, tile = pa_ref.shape[0], pa_ref.shape[1]
    n_chunks = tile // SCAN_CHUNK
    c_len = SCAN_CHUNK
    pair_cols = lambda p: slice(p * GROUP_LANES, (p + 1) * GROUP_LANES)

    @pl.when(t == 0)
    def _():
        for s in range(n_seq):
            for p in range(N_PAIRS):
                state_s[s, p] = jnp.concatenate(
                    [s0_ref[s, 0, HEADS_PER_PAIR * p + j] for j in range(HEADS_PER_PAIR)], axis=1)

    for s in range(n_seq):
        rows = slice(s * tile, (s + 1) * tile)
        k = pa_ref[s, :, D_A:2 * D_A]
        wl = w0_ref[0] + _dot(jnp.tanh(pa_ref[s, :, COL_WD:COL_WD + 2 * R_DECAY]).astype(BF16), w2_ref[0])
        lw_s[rows, :] = -math.exp(-0.5) * _sigmoid(wl)
        a = _sigmoid(a0_ref[0] + _dot(pa_ref[s, :, COL_AD:COL_AD + 2 * R_ICLR].astype(BF16), a2_ref[0]))
        kx = k * kk_ref[0]
        kn = kx * lax.rsqrt(jnp.maximum(_group_sum(kx * kx, bd_ref[...]), 1e-24))
        kt_s[rows, :] = k * (1.0 + (a - 1.0) * ka_ref[0])
        kn_s[rows, :] = kn
        b_s[rows, :] = kn * a

    sign = 1 - 2 * d
    row = lax.broadcasted_iota(jnp.int32, (c_len, c_len), 0)
    col = lax.broadcasted_iota(jnp.int32, (c_len, c_len), 1)
    upto_bf = jnp.where((row - col) * sign >= 0, 1.0, 0.0).astype(BF16)
    row2 = lax.broadcasted_iota(jnp.int32, (c_len, GROUP_LANES), 0)
    col2 = lax.broadcasted_iota(jnp.int32, (c_len, GROUP_LANES), 1)
    tok2 = col2 % HEAD_DIM
    before = (row2 - tok2) * sign > 0
    upto = (row2 - tok2) * sign >= 0
    eye = jnp.where(row2 == tok2, 1.0, 0.0)
    head_of_lane = col2 // HEAD_DIM
    rowb = lax.broadcasted_iota(jnp.int32, (GROUP_LANES, GROUP_LANES), 0)
    colb = lax.broadcasted_iota(jnp.int32, (GROUP_LANES, GROUP_LANES), 1)
    same_head = (rowb // HEAD_DIM) == (colb // HEAD_DIM)
    bdiag = lambda x: _block_diag(x, same_head)

    rh, kap, vv, bh, kh, bq, kq = [], [], [], [], [], [], []
    for s, c in [(s, c) for s in range(n_seq) for c in range(n_chunks)]:
        rows = slice(s * tile + c * c_len, s * tile + (c + 1) * c_len)
        in_rows = slice(c * c_len, (c + 1) * c_len)
        lwc = lw_s[rows, :]
        cum = sum(_dot(upto_bf, piece) for piece in _split_bf16(lwc, 3))
        tot = jnp.sum(lwc, axis=0, keepdims=True)
        e_neg = jnp.exp(-cum)
        e_rest = jnp.exp(tot - cum)
        kap_all = (kn_s[rows, :] * jnp.exp(cum - lwc)).astype(BF16)
        rh_all = pa_ref[s, in_rows, 0:D_A] * jnp.exp(cum)
        kt_c = kt_s[rows, :]
        b_c = b_s[rows, :]
        kh_all = (kt_c * e_neg).astype(BF16)
        bh_all = (b_c * e_neg).astype(BF16)
        kq_all = (kt_c * e_rest).astype(BF16)
        bq_all = (b_c * e_rest).astype(BF16)
        v_all = pa_ref[s, in_rows, 2 * D_A:3 * D_A].astype(BF16)
        pc_s[s * n_chunks + c] = jnp.exp(tot)
        for p in range(N_PAIRS):
            rh.append(rh_all[:, pair_cols(p)])
            kap.append(kap_all[:, pair_cols(p)])
            vv.append(v_all[:, pair_cols(p)])
            bh.append(bh_all[:, pair_cols(p)])
            kh.append(kh_all[:, pair_cols(p)])
            bq.append(bq_all[:, pair_cols(p)])
            kq.append(kq_all[:, pair_cols(p)])
    units = range(len(rh))
    both = [jnp.concatenate([kap[n], rh[n].astype(BF16)], axis=0) for n in units]
    xb = [_dot_nt(both[n], bdiag(bh[n])) for n in units]
    xk = [_dot_nt(both[n], bdiag(kh[n])) for n in units]
    l_b = [jnp.where(before, xb[n][:c_len], 0.0) for n in units]
    a_b = [jnp.where(upto, xb[n][c_len:], 0.0).astype(BF16) for n in units]
    lk_ak = [jnp.concatenate([jnp.where(before, xk[n][:c_len], 0.0), jnp.where(upto, xk[n][c_len:], 0.0)],
                             axis=0).astype(BF16) for n in units]
    lkv_akv = [_dot(lk_ak[n], bdiag(vv[n])) for n in units]
    lkv = [lkv_akv[n][:c_len].astype(BF16) for n in units]
    akv = [lkv_akv[n][c_len:] for n in units]
    inv = [eye - l_b[n] for n in units]
    lp = [l_b[n].astype(BF16) for n in units]
    lp = [_dot(lp[n], bdiag(lp[n])).astype(BF16) for n in units]
    n_sq = int(math.log2(c_len)) - 1
    for j in range(n_sq):
        if j < n_sq - 1:
            res = [_dot(jnp.concatenate([inv[n].astype(BF16), lp[n]], axis=0), bdiag(lp[n])) for n in units]
            inv = [inv[n] + res[n][:c_len] for n in units]
            lp = [res[n][c_len:].astype(BF16) for n in units]
        else:
            inv = [inv[n] + _dot(inv[n].astype(BF16), bdiag(lp[n])) for n in units]
    inv_bf = [inv[n].astype(BF16) for n in units]
    wu = [_dot(inv_bf[n], jnp.concatenate([bdiag(kap[n]), bdiag(lkv[n])], axis=1)).astype(BF16) for n in units]
    w = [wu[n][:, :GROUP_LANES] for n in units]
    u = [wu[n][:, GROUP_LANES:] for n in units]
    ab_wu = [_dot(a_b[n], jnp.concatenate([bdiag(w[n]), bdiag(u[n])], axis=1)) for n in units]
    g = [_diag_blocks(_dot_tn(w[n], bq[n]), head_of_lane) for n in units]
    sa = [_diag_blocks(_dot_tn(jnp.concatenate([vv[n], -u[n]], axis=0),
                               jnp.concatenate([kq[n], bq[n]], axis=0)), head_of_lane) for n in units]
    for n in units:
        gc, p = n // N_PAIRS, n % N_PAIRS
        q_s[gc, p] = rh[n] - ab_wu[n][:, :GROUP_LANES]
        y0_s[gc, p] = akv[n] - ab_wu[n][:, GROUP_LANES:]
        g_s[gc, p] = g[n]
        sa_s[gc, p] = sa[n]

    chains = [(s, p) for s in range(n_seq) for p in range(N_PAIRS)]

    def advance(i, carry):
        c = i + d * (n_chunks - 1 - 2 * i)
        rows = pl.ds(pl.multiple_of(c * c_len, c_len), c_len)
        st = [state_s[s, p] for s, p in chains]
        st_bf = [x.astype(BF16) for x in st]
        sg = [_dot(st_bf[n], bdiag(g_s[s * n_chunks + c, p].astype(BF16))) for n, (s, p) in enumerate(chains)]
        qs = [_dot_nt(q_s[s * n_chunks + c, p].astype(BF16), bdiag(st_bf[n])) for n, (s, p) in enumerate(chains)]
        for n, (s, p) in enumerate(chains):
            gc = s * n_chunks + c
            state_s[s, p] = st[n] * pc_s[gc][:, pair_cols(p)] - sg[n] + sa_s[gc, p]
            y_ref[0, s, rows, pair_cols(p)] = qs[n] + y0_s[gc, p]
        return carry

    lax.fori_loop(0, n_chunks, advance, 0, unroll=True)

    @pl.when(t == n_t - 1)
    def _():
        for s, p in chains:
            st = state_s[s, p]
            for j in range(HEADS_PER_PAIR):
                sout_ref[s, 0, HEADS_PER_PAIR * p + j] = st[:, j * HEAD_DIM:(j + 1) * HEAD_DIM]


def _rwkv_scan(pa, s0, wts, batch, seq_len):
    assert batch % SCAN_SEQS == 0, (batch, SCAN_SEQS)
    tile = min(SCAN_TILE, seq_len)
    n_t = seq_len // tile
    n_chunks = SCAN_SEQS * tile // SCAN_CHUNK

    def time_tile(d, t):
        return t + d * (n_t - 1 - 2 * t)

    per_dir = lambda shape: pl.BlockSpec((1,) + shape, lambda d, b, t: (d,) + (0,) * len(shape))
    state_spec = pl.BlockSpec((SCAN_SEQS, 1, N_HEADS_A, HEAD_DIM, HEAD_DIM), lambda d, b, t: (b, d, 0, 0, 0))
    pair_mat = pltpu.VMEM((n_chunks, N_PAIRS, SCAN_CHUNK, GROUP_LANES), F32)
    stream = pltpu.VMEM((SCAN_SEQS * tile, D_A), F32)
    y, s_new = pl.pallas_call(
        functools.partial(_scan_kernel, n_t=n_t),
        grid=(2, batch // SCAN_SEQS, n_t),
        in_specs=[
            pl.BlockSpec((SCAN_SEQS, tile, COLS_A), lambda d, b, t: (b, time_tile(d, t), 0)), state_spec,
            per_dir((1, D_A)), per_dir((2 * R_DECAY, D_A)), per_dir((1, D_A)), per_dir((2 * R_ICLR, D_A)),
            per_dir((1, D_A)), per_dir((1, D_A)),
            _full((D_A, D_A)),
        ],
        out_specs=[pl.BlockSpec((1, SCAN_SEQS, tile, D_A), lambda d, b, t: (d, b, time_tile(d, t), 0)), state_spec],
        out_shape=[jax.ShapeDtypeStruct((2, batch, seq_len, D_A), F32),
                   jax.ShapeDtypeStruct((batch, 2, N_HEADS_A, HEAD_DIM, HEAD_DIM), F32)],
        scratch_shapes=[
            stream, stream, stream, stream,
            pltpu.VMEM((SCAN_SEQS, N_PAIRS, HEAD_DIM, GROUP_LANES), F32),
            pair_mat, pair_mat, pair_mat, pair_mat,
            pltpu.VMEM((n_chunks, 1, D_A), F32),
        ],
        compiler_params=_params("arbitrary", "arbitrary", "arbitrary"),
        name="rwkv7_scan",
    )(pa.reshape(batch, seq_len, COLS_A), s0, wts["w0"], wts["w2_pad"], wts["a0"], wts["a2_pad"], wts["k_k"],
      wts["k_a"], wts["head_blocks"])
    return y.reshape(2, batch * seq_len, D_A), s_new


def _layernorm(x, g, b):
    mu = jnp.mean(x, axis=-1, keepdims=True)
    xc = x - mu
    var = jnp.mean(xc * xc, axis=-1, keepdims=True)
    return xc * lax.rsqrt(var + LN_EPS) * g + b


def _rmsnorm(x, g):
    return x * lax.rsqrt(jnp.mean(x * x, axis=-1, keepdims=True) + NORM_EPS) * g


def _mixer_kernel(pa_ref, pb_ref, pc_ref, yf_ref, yb_ref,
                  a0_ref, a2_ref, ka_ref, g2_ref, rk_ref, lnxg_ref, lnxb_ref, bd_ref,
                  gng_ref, gnb_ref, ws_ref, bs_ref, betab_ref,
                  cw_ref, cb_ref, cng_ref, cnb_ref, betac_ref,
                  cat_ref, cpad_ref, cphase_ref, *, conv_len):
    bd = bd_ref[...]

    r = pa_ref[:, 0:D_A]
    k = pa_ref[:, D_A:2 * D_A]
    v = pa_ref[:, 2 * D_A:3 * D_A]
    ad = pa_ref[:, COL_AD:COL_AD + 2 * R_ICLR].astype(BF16)
    gd = pa_ref[:, COL_GD:COL_GD + R_GATE]
    a_f = _sigmoid(a0_ref[0] + _dot(ad, a2_ref[0]))
    a_b = _sigmoid(a0_ref[1] + _dot(ad, a2_ref[1]))
    kt_sum = k * (2.0 + (a_f - 1.0) * ka_ref[0] + (a_b - 1.0) * ka_ref[1])
    bonus = _group_sum(r * kt_sum * rk_ref[...], bd) * v
    gate = _dot(_sigmoid(gd).astype(BF16), g2_ref[...])
    y = yf_ref[0] + yb_ref[0]
    mean = _group_sum(y, bd) * (1.0 / HEAD_DIM)
    yc = y - mean
    var = _group_sum(yc * yc, bd) * (1.0 / HEAD_DIM)
    yn = yc * lax.rsqrt(var + LN_X_EPS) * lnxg_ref[...] + lnxb_ref[...]
    cat_ref[:, 0:D_A] = ((yn + bonus) * gate).astype(BF16)

    gb = jax.nn.gelu(pb_ref[...])
    u = gb[:, :D_B]
    vg = _layernorm(gb[:, D_B:], gng_ref[...], gnb_ref[...]).astype(BF16)
    head_of_lane = lax.broadcasted_iota(jnp.int32, (GMLP_CHUNK, D_B), 1) // HEAD_DIM
    sv_chunks = []
    for n in range(TIME_TILE // GMLP_CHUNK):
        vgc = vg[n * GMLP_CHUNK:(n + 1) * GMLP_CHUNK]
        sv = bs_ref[...]
        for g in range(N_GROUPS_B):
            sv = sv + jnp.where(head_of_lane == g, _dot(ws_ref[g], vgc), 0.0)
        sv_chunks.append(sv)
    sv = jnp.concatenate(sv_chunks, axis=0)
    cat_ref[:, D_A:D_A + D_B] = _rmsnorm(u * sv, betab_ref[...]).astype(BF16)

    pc = pc_ref[...]
    gl = pc[:, :D_C] * _sigmoid(pc[:, D_C:])
    stride = conv_len + 2 * CONV_HALO
    zeros = jnp.zeros((CONV_HALO, D_C), F32)
    convs = []
    for q in range(TIME_TILE // conv_len):
        base = q * stride
        cpad_ref[base:base + CONV_HALO, :] = zeros
        cpad_ref[base + CONV_HALO:base + CONV_HALO + conv_len, :] = gl[q * conv_len:(q + 1) * conv_len]
        cpad_ref[base + CONV_HALO + conv_len:base + stride, :] = zeros
    total = (TIME_TILE // conv_len) * stride
    for s in range(1, SUBLANES):
        cphase_ref[s, 0:total - SUBLANES, :] = cpad_ref[s:s + total - SUBLANES, :]
    for q in range(TIME_TILE // conv_len):
        base = q * stride + CONV_HALO - CONV_PAD
        acc = jnp.zeros((conv_len, D_C), F32) + cb_ref[...]
        for j in range(CONV_K):
            phase = (base + j) % SUBLANES
            first = base + j - phase
            if phase == 0:
                taps = cpad_ref[first:first + conv_len, :]
            else:
                taps = cphase_ref[phase, first:first + conv_len, :]
            acc = acc + cw_ref[j:j + 1, :] * taps
        convs.append(acc)
    conv = jnp.concatenate(convs, axis=0) if len(convs) > 1 else convs[0]
    z = _layernorm(conv, cng_ref[...], cnb_ref[...])
    z = z * _sigmoid(z)
    cat_ref[:, D_A + D_B:] = _rmsnorm(z, betac_ref[...]).astype(BF16)


def _token_mixers(pa, pb, pc, y_scan, wts, conv_len):
    n = pa.shape[0]
    tile = lambda width: pl.BlockSpec((TIME_TILE, width), lambda i: (i, 0))
    y_dir = lambda d: pl.BlockSpec((1, TIME_TILE, D_A), lambda i: (d, i, 0))
    n_conv = TIME_TILE // conv_len
    return pl.pallas_call(
        functools.partial(_mixer_kernel, conv_len=conv_len),
        grid=(n // TIME_TILE,),
        in_specs=[
            tile(COLS_A), tile(COLS_B), tile(COLS_C), y_dir(0), y_dir(1),
            _full((2, 1, D_A)), _full((2, 2 * R_ICLR, D_A)), _full((2, 1, D_A)),
            _full((R_GATE, D_A)), _full((1, D_A)), _full((1, D_A)), _full((1, D_A)), _full((D_A, D_A)),
            _full((1, D_B)), _full((1, D_B)), _full((N_GROUPS_B, GMLP_CHUNK, GMLP_CHUNK)),
            _full((GMLP_CHUNK, D_B)), _full((1, D_B)),
            _full((CONV_K, D_C)), _full((1, D_C)), _full((1, D_C)), _full((1, D_C)), _full((1, D_C)),
        ],
        out_specs=tile(D_MODEL),
        out_shape=jax.ShapeDtypeStruct((n, D_MODEL), BF16),
        scratch_shapes=[pltpu.VMEM((n_conv * (conv_len + 2 * CONV_HALO), D_C), F32),
                        pltpu.VMEM((SUBLANES, n_conv * (conv_len + 2 * CONV_HALO), D_C), F32)],
        compiler_params=_params("arbitrary"),
        name="token_mixers",
    )(pa, pb, pc, y_scan, y_scan,
      wts["a0"], wts["a2_pad"], wts["k_a"], wts["g2"], wts["r_k"], wts["lnx_g"], wts["lnx_b"],
      wts["head_blocks"], wts["gmlp_norm_g"], wts["gmlp_norm_b"], wts["gmlp_ws"], wts["gmlp_bias"],
      wts["beta_b"], wts["conv_w"], wts["conv_b"], wts["conv_norm_g"], wts["conv_norm_b"], wts["beta_c"])


def _top2_sum(a, b, c, d):
    hi1, lo1 = jnp.maximum(a, b), jnp.minimum(a, b)
    hi2, lo2 = jnp.maximum(c, d), jnp.minimum(c, d)
    return jnp.maximum(hi1, hi2) + jnp.maximum(jnp.minimum(hi1, hi2), jnp.maximum(lo1, lo2))


def _router_gates(logits_t, bias_ref):
    m = jnp.max(logits_t, axis=0, keepdims=True)
    e = jnp.exp(logits_t - m)
    probs = e / jnp.sum(e, axis=0, keepdims=True)
    sel = probs + bias_ref[...]
    p_row = [probs[i:i + 1, :] for i in range(N_EXPERTS)]
    s_row = [sel[i:i + 1, :] for i in range(N_EXPERTS)]
    best_val = None
    best = None
    for g in range(N_EXPERT_GROUPS):
        score = _top2_sum(*s_row[g * EXPERTS_PER_GROUP:(g + 1) * EXPERTS_PER_GROUP])
        if g == 0:
            best_val, best = score, jnp.zeros_like(score, dtype=jnp.int32)
        else:
            better = score > best_val
            best_val = jnp.where(better, score, best_val)
            best = jnp.where(better, g, best)
    chosen = []
    for i in range(N_EXPERTS):
        g = i // EXPERTS_PER_GROUP
        rank = jnp.zeros_like(best)
        for j in range(g * EXPERTS_PER_GROUP, (g + 1) * EXPERTS_PER_GROUP):
            if j == i:
                continue
            ahead = (s_row[j] >= s_row[i]) if j < i else (s_row[j] > s_row[i])
            rank = rank + ahead.astype(jnp.int32)
        chosen.append((best == g) & (rank < 2))
    picked = [jnp.where(chosen[i], p_row[i], 0.0) for i in range(N_EXPERTS)]
    denom = picked[0]
    for i in range(1, N_EXPERTS):
        denom = denom + picked[i]
    gates = jnp.concatenate([p / denom for p in picked], axis=0)
    group_rows = [jnp.where(best == g, 1.0, 0.0) for g in range(N_EXPERT_GROUPS)]
    group_rows += [jnp.zeros_like(group_rows[0])] * (SUBLANES - N_EXPERT_GROUPS)
    return gates, jnp.concatenate(group_rows, axis=0)


def _outproj_kernel(cat_ref, x_ref, mod_ref, wout_ref, g_ref, wr_hi_ref, wr_lo_ref, br_ref,
                    x1_ref, h2_ref, gates_ref, group_ref):
    mod = mod_ref[0]
    out = _dot(cat_ref[...], wout_ref[...])
    x1 = x_ref[...] + mod[:, 2 * D_MODEL:3 * D_MODEL] * out
    x1_ref[...] = x1
    h2 = _modulated_rmsnorm(x1, g_ref[...], mod[:, 3 * D_MODEL:4 * D_MODEL], mod[:, 4 * D_MODEL:5 * D_MODEL])
    h_hi, h_lo = _split_bf16(h2, 2)
    h2_ref[...] = h_hi
    by_hi = _dot_nt(jnp.concatenate([wr_hi_ref[...], wr_lo_ref[...]], axis=0), h_hi)
    logits_t = by_hi[:N_EXPERTS] + _dot_nt(wr_hi_ref[...], h_lo) + by_hi[N_EXPERTS:]
    gates_ref[...], group_ref[...] = _router_gates(logits_t, br_ref)


def _out_projection(cat, x, mod, w_out, norm_g, wr_hi, wr_lo, b_router, seq_len, per_batch):
    n = x.shape[0]
    row = lambda width: pl.BlockSpec((ROW_TILE, width), lambda i: (i, 0))
    return pl.pallas_call(
        _outproj_kernel,
        grid=(n // ROW_TILE,),
        in_specs=[row(D_MODEL), row(D_MODEL), _mod_spec(seq_len, ROW_TILE, per_batch),
                  _full((D_MODEL, D_MODEL)), _full((1, D_MODEL)),
                  _full((N_EXPERTS, D_MODEL)), _full((N_EXPERTS, D_MODEL)), _full((N_EXPERTS, 1))],
        out_specs=[row(D_MODEL), row(D_MODEL), pl.BlockSpec((N_EXPERTS, ROW_TILE), lambda i: (0, i)),
                   pl.BlockSpec((SUBLANES, ROW_TILE), lambda i: (0, i))],
        out_shape=[jax.ShapeDtypeStruct((n, D_MODEL), F32), jax.ShapeDtypeStruct((n, D_MODEL), BF16),
                   jax.ShapeDtypeStruct((N_EXPERTS, n), F32), jax.ShapeDtypeStruct((SUBLANES, n), F32)],
        compiler_params=_params("arbitrary"),
        name="out_projection_router",
    )(cat, x, mod, w_out, norm_g, wr_hi, wr_lo, b_router)


def _moe_kernel(start_ref, h_ref, gates_ref, grp_t_ref, earlier_ref, x1_ref, mod_ref, wgu_ref,
                wdn_ref, fg_ref, o_ref, xs_ref, gs_ref, ys_ref, perm_ref, *, final_norm):
    blk = pl.program_id(0)
    grp = pl.program_id(1)
    tm = h_ref.shape[0]

    @pl.when(grp == 0)
    def _():
        earlier = earlier_ref[...]
        grp_t = grp_t_ref[...]
        rank_t = _dot(grp_t.astype(BF16), earlier)
        slot_t = jnp.zeros((1, tm), F32)
        for g in range(N_EXPERT_GROUPS):
            first = start_ref[blk, g].astype(F32)
            slot_t = slot_t + grp_t[g:g + 1, :] * (rank_t[g:g + 1, :] + first)
        slot_rows = lax.broadcasted_iota(jnp.int32, (tm, tm), 0)
        to_slots = jnp.where(slot_rows == slot_t.astype(jnp.int32), 1.0, 0.0).astype(BF16)
        perm_ref[...] = to_slots
        xs_ref[...] = _dot(to_slots, h_ref[...]).astype(BF16)
        pieces = _dot(to_slots, jnp.concatenate(_split_bf16(gates_ref[...], 3), axis=1))
        gs_ref[...] = (pieces[:, 0:N_EXPERTS] + pieces[:, N_EXPERTS:2 * N_EXPERTS]
                       + pieces[:, 2 * N_EXPERTS:3 * N_EXPERTS])
        ys_ref[...] = jnp.zeros_like(ys_ref)

    lo = start_ref[blk, grp] // MOE_CHUNK
    hi = (start_ref[blk, grp + 1] + MOE_CHUNK - 1) // MOE_CHUNK
    lane = lax.broadcasted_iota(jnp.int32, (MOE_CHUNK, N_EXPERTS), 1)

    def chunk(j, carry):
        rows = pl.ds(pl.multiple_of(j * MOE_CHUNK, MOE_CHUNK), MOE_CHUNK)
        x = xs_ref[rows, :]
        gates = gs_ref[rows, :]
        acts = []
        for e in range(EXPERTS_PER_GROUP):
            gate = jnp.sum(jnp.where(lane == grp * EXPERTS_PER_GROUP + e, gates, 0.0), axis=1, keepdims=True)
            gu = _dot(x, wgu_ref[0, e])
            g = gu[:, :D_EXPERT]
            acts.append((g * _sigmoid(g) * gu[:, D_EXPERT:] * gate).astype(BF16))
        ys_ref[rows, :] += _dot(jnp.concatenate(acts, axis=1), wdn_ref[0])
        return carry

    lax.fori_loop(lo, hi, chunk, 0)

    @pl.when(grp == N_EXPERT_GROUPS - 1)
    def _():
        moe = _dot_tn(perm_ref[...], ys_ref[...].astype(BF16))
        x2 = x1_ref[...] + mod_ref[0][:, 5 * D_MODEL:6 * D_MODEL] * moe
        if final_norm:
            x2 = _rmsnorm(x2, fg_ref[...])
        o_ref[...] = x2


def _mixture_of_experts(h2, gates_t, group_t, x1, mod, w_gu, w_down, earlier, final_g, seq_len, per_batch,
                        final_norm):
    n = x1.shape[0]
    n_blocks = n // MOE_ROW_TILE
    counts = group_t[:N_EXPERT_GROUPS].reshape(N_EXPERT_GROUPS, n_blocks, MOE_ROW_TILE).sum(axis=-1)
    starts = jnp.concatenate([jnp.zeros((1, n_blocks), F32), jnp.cumsum(counts, axis=0)], axis=0)
    starts = starts.T.astype(jnp.int32)
    row = lambda width: pl.BlockSpec((MOE_ROW_TILE, width), lambda i, g, s: (i, 0))
    col = lambda height: pl.BlockSpec((height, MOE_ROW_TILE), lambda i, g, s: (0, i))
    order = pl.BlockSpec((MOE_ROW_TILE, MOE_ROW_TILE), lambda i, g, s: (0, 0))
    mod_spec = _mod_spec(seq_len, MOE_ROW_TILE, per_batch)
    grid_spec = pltpu.PrefetchScalarGridSpec(
        num_scalar_prefetch=1,
        grid=(n_blocks, N_EXPERT_GROUPS),
        in_specs=[row(D_MODEL), row(N_EXPERTS), col(SUBLANES), order, row(D_MODEL), mod_spec,
                  pl.BlockSpec((1, EXPERTS_PER_GROUP, D_MODEL, 2 * D_EXPERT), lambda i, g, s: (g, 0, 0, 0)),
                  pl.BlockSpec((1, EXPERTS_PER_GROUP * D_EXPERT, D_MODEL), lambda i, g, s: (g, 0, 0)),
                  pl.BlockSpec((1, D_MODEL), lambda i, g, s: (0, 0))],
        out_specs=row(D_MODEL),
        scratch_shapes=[pltpu.VMEM((MOE_ROW_TILE, D_MODEL), BF16), pltpu.VMEM((MOE_ROW_TILE, N_EXPERTS), F32),
                        pltpu.VMEM((MOE_ROW_TILE, D_MODEL), F32), pltpu.VMEM((MOE_ROW_TILE, MOE_ROW_TILE), BF16)],
    )
    return pl.pallas_call(
        functools.partial(_moe_kernel, final_norm=final_norm),
        grid_spec=grid_spec,
        out_shape=jax.ShapeDtypeStruct((n, D_MODEL), F32),
        compiler_params=_params("arbitrary", "arbitrary"),
        name="mixture_of_experts",
    )(starts, h2, gates_t.T, group_t, earlier, x1, mod, w_gu, w_down, final_g)


def _pad_low_rank(w, rank):
    z = jnp.zeros_like(w[0])
    return jnp.stack([jnp.concatenate([w[0], z], axis=0), jnp.concatenate([z, w[1]], axis=0)]).astype(BF16)


def _layer_weights(l, head_blocks, mu_shift, w0, w2, a0, a2, k_k, k_a, g2, r_k, lnx_g, lnx_b, gmlp_norm_g,
                   gmlp_norm_b, gmlp_ws, gmlp_bs, beta_b, conv_w, conv_b, conv_norm_g, conv_norm_b, beta_c):
    return dict(
        mu_shift=mu_shift[l].reshape(1, COLS_A),
        w0=w0[l].reshape(2, 1, D_A), w2_pad=_pad_low_rank(w2[l], R_DECAY),
        a0=a0[l].reshape(2, 1, D_A), a2_pad=_pad_low_rank(a2[l], R_ICLR),
        k_k=k_k[l].reshape(2, 1, D_A), k_a=k_a[l].reshape(2, 1, D_A),
        g2=g2[l].astype(BF16), r_k=r_k[l].reshape(1, D_A),
        lnx_g=lnx_g[l].reshape(1, D_A), lnx_b=lnx_b[l].reshape(1, D_A), head_blocks=head_blocks,
        gmlp_norm_g=gmlp_norm_g[l].reshape(1, D_B), gmlp_norm_b=gmlp_norm_b[l].reshape(1, D_B),
        gmlp_ws=gmlp_ws[l].astype(BF16), gmlp_bias=jnp.repeat(gmlp_bs[l].T, HEAD_DIM, axis=1),
        beta_b=beta_b[l].reshape(1, D_B),
        conv_w=conv_w[l], conv_b=conv_b[l].reshape(1, D_C),
        conv_norm_g=conv_norm_g[l].reshape(1, D_C), conv_norm_b=conv_norm_b[l].reshape(1, D_C),
        beta_c=beta_c[l].reshape(1, D_C),
    )


def kernel(x_prompt, x_sample, state_rwkv, c, c_ctx, norm1_g, norm2_g, ada_w, ada_b, w_in, mu_shift, w0, w2, a0, a2, k_k, k_a, g2, r_k, lnx_g, lnx_b, gmlp_norm_g, gmlp_norm_b, gmlp_ws, gmlp_bs, beta_b, conv_w, conv_b, conv_norm_g, conv_norm_b, beta_c, w_out, w_router, b_router, moe_w_gu, moe_w_down, final_g):
    batch_p, seq_p, _ = x_prompt.shape
    batch_s, seq_s, _ = x_sample.shape

    n_cond = 1 + batch_s
    cond_rows = -(-n_cond // SUBLANES) * SUBLANES
    c_all = jnp.concatenate([c_ctx[None, :], c, jnp.zeros((cond_rows - n_cond, D_MODEL), F32)], axis=0)
    mod_all = _modulation(c_all, ada_w, ada_b)

    head_id = jnp.arange(D_A) // HEAD_DIM
    head_blocks = (head_id[:, None] == head_id[None, :]).astype(BF16)
    wr_t = w_router.T
    wr_hi = wr_t.astype(BF16)
    wr_lo = (wr_t - wr_hi.astype(F32)).astype(BF16)
    b_r = b_router.reshape(N_EXPERTS, 1)
    final_g2 = final_g.reshape(1, D_MODEL)
    layer_wts = [_layer_weights(l, head_blocks, mu_shift, w0, w2, a0, a2, k_k, k_a, g2, r_k, lnx_g, lnx_b,
                                gmlp_norm_g, gmlp_norm_b, gmlp_ws, gmlp_bs, beta_b, conv_w, conv_b, conv_norm_g,
                                conv_norm_b, beta_c) for l in range(DEPTH)]
    w_in_bf = [w_in[l].astype(BF16) for l in range(DEPTH)]
    w_out_bf = [w_out[l].astype(BF16) for l in range(DEPTH)]
    w_gu_bf = [moe_w_gu[l].astype(BF16).reshape(N_EXPERT_GROUPS, EXPERTS_PER_GROUP, D_MODEL, 2 * D_EXPERT)
               for l in range(DEPTH)]
    w_down_bf = [moe_w_down[l].astype(BF16).reshape(N_EXPERT_GROUPS, EXPERTS_PER_GROUP * D_EXPERT, D_MODEL)
                 for l in range(DEPTH)]
    token_id = jnp.arange(MOE_ROW_TILE)
    earlier = (token_id[:, None] < token_id[None, :]).astype(BF16)

    groups = [
        dict(x=x_prompt.reshape(batch_p * seq_p, D_MODEL), batch=batch_p, seq=seq_p, conv_len=seq_p,
             per_batch=False, s0=None, mod_rows=slice(0, 1)),
        dict(x=x_sample.reshape(batch_s * seq_s, D_MODEL), batch=batch_s, seq=seq_s, conv_len=GRID_W,
             per_batch=True, s0=state_rwkv, mod_rows=slice(1, 1 + batch_s)),
    ]
    ctx_states = []
    outputs = []
    for grp in groups:
        x = grp["x"]
        batch, seq, per_batch = grp["batch"], grp["seq"], grp["per_batch"]
        for l in range(DEPTH):
            mod = mod_all[l, grp["mod_rows"]][:, None, :]
            wts = layer_wts[l]
            pa, pb, pc = _in_projection(x, mod, norm1_g[l].reshape(1, D_MODEL), w_in_bf[l], wts["mu_shift"], seq,
                                        per_batch)
            if grp["s0"] is None:
                s0 = jnp.zeros((batch, 2, N_HEADS_A, HEAD_DIM, HEAD_DIM), F32)
            else:
                s0 = grp["s0"][:, l]
            y_scan, s_new = _rwkv_scan(pa, s0, wts, batch, seq)
            cat = _token_mixers(pa, pb, pc, y_scan, wts, grp["conv_len"])
            x1, h2, gates_t, group_t = _out_projection(cat, x, mod, w_out_bf[l], norm2_g[l].reshape(1, D_MODEL),
                                                       wr_hi, wr_lo, b_r, seq, per_batch)
            x = _mixture_of_experts(h2, gates_t, group_t, x1, mod, w_gu_bf[l], w_down_bf[l], earlier,
                                    final_g2, seq, per_batch, final_norm=(l == DEPTH - 1))
            if grp["s0"] is None:
                ctx_states.append(s_new)
        outputs.append(x.reshape(batch, seq, D_MODEL))
    return (outputs[0], outputs[1], jnp.stack(ctx_states, axis=1))
```

```python
import functools
import math

import jax
import jax.numpy as jnp
from jax import lax
from jax.experimental import pallas as pl
from jax.experimental.pallas import tpu as pltpu

D_MODEL = 1024
DEPTH = 2
GRID_W = 64
HEAD_DIM = 64
D_A = 512
N_HEADS_A = 8
R_DECAY = 64
R_ICLR = 64
R_GATE = 128
D_B = 256
N_GROUPS_B = 4
GMLP_CHUNK = 128
D_C = 256
CONV_K = 31
CONV_PAD = CONV_K // 2
N_EXPERTS = 16
N_EXPERT_GROUPS = 4
EXPERTS_PER_GROUP = 4
D_EXPERT = 256
NORM_EPS = 1e-6
LN_EPS = 1e-5
LN_X_EPS = 64e-5
COLS_A = 3 * D_A + 2 * R_DECAY + 2 * R_ICLR + R_GATE
COLS_B = 2 * D_B
COLS_C = 2 * D_C
D_PROJ = COLS_A + COLS_B + COLS_C
COL_WD = 3 * D_A
COL_AD = COL_WD + 2 * R_DECAY
COL_GD = COL_AD + 2 * R_ICLR

SUBLANES = 8
GROUP_LANES = 128
TIME_TILE = 256
SCAN_TILE = 256
SCAN_SEQS = 2
SCAN_CHUNK = 64
HEADS_PER_PAIR = GROUP_LANES // HEAD_DIM
N_PAIRS = N_HEADS_A // HEADS_PER_PAIR
CONV_HALO = 16
ROW_TILE = 512
INPROJ_ROW_TILE = 512
SHIFT_COL_CHUNK = 256
MOE_ROW_TILE = 1024
MOE_CHUNK = 128
MOD_COL_TILE = 1536
VMEM_LIMIT = 56 * 1024 * 1024

BF16 = jnp.bfloat16
F32 = jnp.float32

_NT = (((1,), (1,)), ((), ()))
_TN = (((0,), (0,)), ((), ()))


def _dot(a, b):
    return jnp.dot(a, b, preferred_element_type=F32)


def _dot_nt(a, b):
    return lax.dot_general(a, b, _NT, preferred_element_type=F32)


def _dot_tn(a, b):
    return lax.dot_general(a, b, _TN, preferred_element_type=F32)


def _split_bf16(x, parts):
    out = []
    for _ in range(parts):
        p = x.astype(BF16)
        out.append(p)
        x = x - p.astype(F32)
    return out


def _sigmoid(x):
    return 1.0 / (1.0 + jnp.exp(-x))


def _group_sum(x, bd):
    hi, lo = _split_bf16(x, 2)
    return _dot(hi, bd) + _dot(lo, bd)


def _full(shape):
    zeros = (0,) * len(shape)
    return pl.BlockSpec(shape, lambda *_: zeros)


def _params(*sem):
    return pltpu.CompilerParams(dimension_semantics=sem, vmem_limit_bytes=VMEM_LIMIT)


def _mod_kernel(c_ref, w_ref, b_ref, o_ref):
    c = c_ref[...]
    s = c * _sigmoid(c)
    o_ref[0] = _dot(s.astype(BF16), w_ref[0].astype(BF16)) + b_ref[0]


def _modulation(c_all, ada_w, ada_b):
    rows = c_all.shape[0]
    n_col = ada_w.shape[-1] // MOD_COL_TILE
    return pl.pallas_call(
        _mod_kernel,
        grid=(DEPTH, n_col),
        in_specs=[
            _full((rows, D_MODEL)),
            pl.BlockSpec((1, D_MODEL, MOD_COL_TILE), lambda l, j: (l, 0, j)),
            pl.BlockSpec((1, 1, MOD_COL_TILE), lambda l, j: (l, 0, j)),
        ],
        out_specs=pl.BlockSpec((1, rows, MOD_COL_TILE), lambda l, j: (l, 0, j)),
        out_shape=jax.ShapeDtypeStruct((DEPTH, rows, ada_w.shape[-1]), F32),
        compiler_params=_params("arbitrary", "arbitrary"),
        name="adaln_modulation",
    )(c_all, ada_w, ada_b.reshape(DEPTH, 1, -1))


def _modulated_rmsnorm(x, gain, shift, scale):
    y = x * lax.rsqrt(jnp.mean(x * x, axis=-1, keepdims=True) + NORM_EPS)
    return y * gain * (1.0 + scale) + shift


def _inproj_kernel(x_ref, xprev_ref, xnext_ref, mod_ref, g_ref, w_ref, mu_ref, pa_ref, pb_ref, pc_ref, pad_ref,
                   *, seq_len):
    tm = x_ref.shape[0]
    mod = mod_ref[0]
    norm = lambda x: _modulated_rmsnorm(x, g_ref[...], mod[:, 0:D_MODEL], mod[:, D_MODEL:2 * D_MODEL])
    h = norm(x_ref[...]).astype(BF16)
    h_halo = jnp.concatenate([norm(xprev_ref[...]), norm(xnext_ref[...])], axis=0).astype(BF16)
    pos = (pl.program_id(0) * tm + lax.broadcasted_iota(jnp.int32, (tm, 1), 0)) % seq_len
    at_start = pos == 0
    at_end = pos == seq_len - 1
    for start in range(0, COLS_A, SHIFT_COL_CHUNK):
        cols = slice(start, min(start + SHIFT_COL_CHUNK, COLS_A))
        pa = _dot(h, w_ref[:, cols])
        p_halo = _dot(h_halo, w_ref[:, cols])
        pad_ref[0:SUBLANES, cols] = p_halo[:SUBLANES]
        pad_ref[SUBLANES:SUBLANES + tm, cols] = pa
        pad_ref[SUBLANES + tm:2 * SUBLANES + tm, cols] = p_halo[SUBLANES:]
        prev = jnp.where(at_start, 0.0, pad_ref[SUBLANES - 1:SUBLANES - 1 + tm, cols])
        nxt = jnp.where(at_end, 0.0, pad_ref[SUBLANES + 1:SUBLANES + 1 + tm, cols])
        pa_ref[:, cols] = pa + mu_ref[:, cols] * (0.5 * (prev + nxt) - pa)
    pb_ref[...] = _dot(h, w_ref[:, COLS_A:COLS_A + COLS_B])
    pc_ref[...] = _dot(h, w_ref[:, COLS_A + COLS_B:])


def _mod_spec(seq_len, row_tile, per_batch):
    if per_batch:
        return pl.BlockSpec((1, 1, 6 * D_MODEL), lambda i, *_: (i * row_tile // seq_len, 0, 0))
    return pl.BlockSpec((1, 1, 6 * D_MODEL), lambda i, *_: (0, 0, 0))


def _in_projection(x, mod, norm_g, w_in, mu_shift, seq_len, per_batch):
    n = x.shape[0]
    tm = INPROJ_ROW_TILE
    per_tile = tm // SUBLANES
    last_block = n // SUBLANES - 1
    row = lambda width: pl.BlockSpec((tm, width), lambda i: (i, 0))
    prev_spec = pl.BlockSpec((SUBLANES, D_MODEL), lambda i: (jnp.maximum(i * per_tile - 1, 0), 0))
    next_spec = pl.BlockSpec((SUBLANES, D_MODEL), lambda i: (jnp.minimum((i + 1) * per_tile, last_block), 0))
    return pl.pallas_call(
        functools.partial(_inproj_kernel, seq_len=seq_len),
        grid=(n // tm,),
        in_specs=[row(D_MODEL), prev_spec, next_spec, _mod_spec(seq_len, tm, per_batch), _full((1, D_MODEL)),
                  _full((D_MODEL, D_PROJ)), _full((1, COLS_A))],
        out_specs=[row(COLS_A), row(COLS_B), row(COLS_C)],
        out_shape=[jax.ShapeDtypeStruct((n, COLS_A), F32), jax.ShapeDtypeStruct((n, COLS_B), F32),
                   jax.ShapeDtypeStruct((n, COLS_C), F32)],
        scratch_shapes=[pltpu.VMEM((tm + 2 * SUBLANES, COLS_A), F32)],
        compiler_params=_params("arbitrary"),
        name="in_projection",
    )(x, x, x, mod, norm_g, w_in, mu_shift)


def _block_diag(x, same_head):
    return jnp.where(same_head, jnp.concatenate([x] * HEADS_PER_PAIR, axis=0), jnp.zeros((), x.dtype))


def _diag_blocks(x, head_of_lane):
    out = x[:HEAD_DIM]
    for j in range(1, HEADS_PER_PAIR):
        out = jnp.where(head_of_lane == j, x[j * HEAD_DIM:(j + 1) * HEAD_DIM], out)
    return out


def _scan_kernel(pa_ref, s0_ref, w0_ref, w2_ref, a0_ref, a2_ref, kk_ref, ka_ref, bd_ref, y_ref, sout_ref, kt_ref,
                 lw_s, kt_s, kn_s, b_s, state_s, g_s, sa_s, q_s, y0_s, pc_s, *, n_t):
    d = pl.program_id(0)
    t = pl.program_id(2)
    n_seq, tile = pa_ref.shape[0], pa_ref.shape[1]
    n_chunks = tile // SCAN_CHUNK
    c_len = SCAN_CHUNK
    pair_cols = lambda p: slice(p * GROUP_LANES, (p + 1) * GROUP_LANES)

    @pl.when(t == 0)
    def _():
        for s in range(n_seq):
            for p in range(N_PAIRS):
                state_s[s, p] = jnp.concatenate(
                    [s0_ref[s, 0, HEADS_PER_PAIR * p + j] for j in range(HEADS_PER_PAIR)], axis=1)

    for s in range(n_seq):
        rows = slice(s * tile, (s + 1) * tile)
        k = pa_ref[s, :, D_A:2 * D_A]
        wl = w0_ref[0] + _dot(jnp.tanh(pa_ref[s, :, COL_WD:COL_WD + 2 * R_DECAY]).astype(BF16), w2_ref[0])
        lw_s[rows, :] = -math.exp(-0.5) * _sigmoid(wl)
        a = _sigmoid(a0_ref[0] + _dot(pa_ref[s, :, COL_AD:COL_AD + 2 * R_ICLR].astype(BF16), a2_ref[0]))
        kx = k * kk_ref[0]
        kn = kx * lax.rsqrt(jnp.maximum(_group_sum(kx * kx, bd_ref[...]), 1e-24))
        kt = k * (1.0 + (a - 1.0) * ka_ref[0])
        kt_s[rows, :] = kt
        kt_ref[0, s] = kt
        kn_s[rows, :] = kn
        b_s[rows, :] = kn * a

    sign = 1 - 2 * d
    row = lax.broadcasted_iota(jnp.int32, (c_len, c_len), 0)
    col = lax.broadcasted_iota(jnp.int32, (c_len, c_len), 1)
    upto_bf = jnp.where((row - col) * sign >= 0, 1.0, 0.0).astype(BF16)
    row2 = lax.broadcasted_iota(jnp.int32, (c_len, GROUP_LANES), 0)
    col2 = lax.broadcasted_iota(jnp.int32, (c_len, GROUP_LANES), 1)
    tok2 = col2 % HEAD_DIM
    before = (row2 - tok2) * sign > 0
    upto = (row2 - tok2) * sign >= 0
    eye = jnp.where(row2 == tok2, 1.0, 0.0)
    head_of_lane = col2 // HEAD_DIM
    rowb = lax.broadcasted_iota(jnp.int32, (GROUP_LANES, GROUP_LANES), 0)
    colb = lax.broadcasted_iota(jnp.int32, (GROUP_LANES, GROUP_LANES), 1)
    same_head = (rowb // HEAD_DIM) == (colb // HEAD_DIM)
    bdiag = lambda x: _block_diag(x, same_head)

    rh, kap, vv, bh, kh, bq, kq = [], [], [], [], [], [], []
    for s, c in [(s, c) for s in range(n_seq) for c in range(n_chunks)]:
        rows = slice(s * tile + c * c_len, s * tile + (c + 1) * c_len)
        in_rows = slice(c * c_len, (c + 1) * c_len)
        lwc = lw_s[rows, :]
        cum = sum(_dot(upto_bf, piece) for piece in _split_bf16(lwc, 3))
        tot = jnp.sum(lwc, axis=0, keepdims=True)
        e_neg = jnp.exp(-cum)
        e_rest = jnp.exp(tot - cum)
        kap_all = (kn_s[rows, :] * jnp.exp(cum - lwc)).astype(BF16)
        rh_all = pa_ref[s, in_rows, 0:D_A] * jnp.exp(cum)
        kt_c = kt_s[rows, :]
        b_c = b_s[rows, :]
        kh_all = (kt_c * e_neg).astype(BF16)
        bh_all = (b_c * e_neg).astype(BF16)
        kq_all = (kt_c * e_rest).astype(BF16)
        bq_all = (b_c * e_rest).astype(BF16)
        v_all = pa_ref[s, in_rows, 2 * D_A:3 * D_A].astype(BF16)
        pc_s[s * n_chunks + c] = jnp.exp(tot)
        for p in range(N_PAIRS):
            rh.append(rh_all[:, pair_cols(p)])
            kap.append(kap_all[:, pair_cols(p)])
            vv.append(v_all[:, pair_cols(p)])
            bh.append(bh_all[:, pair_cols(p)])
            kh.append(kh_all[:, pair_cols(p)])
            bq.append(bq_all[:, pair_cols(p)])
            kq.append(kq_all[:, pair_cols(p)])
    units = range(len(rh))
    both = [jnp.concatenate([kap[n], rh[n].astype(BF16)], axis=0) for n in units]
    xb = [_dot_nt(both[n], bdiag(bh[n])) for n in units]
    xk = [_dot_nt(both[n], bdiag(kh[n])) for n in units]
    l_b = [jnp.where(before, xb[n][:c_len], 0.0) for n in units]
    a_b = [jnp.where(upto, xb[n][c_len:], 0.0).astype(BF16) for n in units]
    lk_ak = [jnp.concatenate([jnp.where(before, xk[n][:c_len], 0.0), jnp.where(upto, xk[n][c_len:], 0.0)],
                             axis=0).astype(BF16) for n in units]
    lkv_akv = [_dot(lk_ak[n], bdiag(vv[n])) for n in units]
    lkv = [lkv_akv[n][:c_len].astype(BF16) for n in units]
    akv = [lkv_akv[n][c_len:] for n in units]
    inv = [eye - l_b[n] for n in units]
    lp = [l_b[n].astype(BF16) for n in units]
    lp = [_dot(lp[n], bdiag(lp[n])).astype(BF16) for n in units]
    n_sq = int(math.log2(c_len)) - 1
    for j in range(n_sq):
        if j < n_sq - 1:
            res = [_dot(jnp.concatenate([inv[n].astype(BF16), lp[n]], axis=0), bdiag(lp[n])) for n in units]
            inv = [inv[n] + res[n][:c_len] for n in units]
            lp = [res[n][c_len:].astype(BF16) for n in units]
        else:
            inv = [inv[n] + _dot(inv[n].astype(BF16), bdiag(lp[n])) for n in units]
    inv_bf = [inv[n].astype(BF16) for n in units]
    wu = [_dot(inv_bf[n], jnp.concatenate([bdiag(kap[n]), bdiag(lkv[n])], axis=1)).astype(BF16) for n in units]
    w = [wu[n][:, :GROUP_LANES] for n in units]
    u = [wu[n][:, GROUP_LANES:] for n in units]
    ab_wu = [_dot(a_b[n], jnp.concatenate([bdiag(w[n]), bdiag(u[n])], axis=1)) for n in units]
    g = [_diag_blocks(_dot_tn(w[n], bq[n]), head_of_lane) for n in units]
    sa = [_diag_blocks(_dot_tn(jnp.concatenate([vv[n], -u[n]], axis=0),
                               jnp.concatenate([kq[n], bq[n]], axis=0)), head_of_lane) for n in units]
    for n in units:
        gc, p = n // N_PAIRS, n % N_PAIRS
        q_s[gc, p] = rh[n] - ab_wu[n][:, :GROUP_LANES]
        y0_s[gc, p] = akv[n] - ab_wu[n][:, GROUP_LANES:]
        g_s[gc, p] = g[n]
        sa_s[gc, p] = sa[n]

    chains = [(s, p) for s in range(n_seq) for p in range(N_PAIRS)]

    def advance(i, carry):
        c = i + d * (n_chunks - 1 - 2 * i)
        rows = pl.ds(pl.multiple_of(c * c_len, c_len), c_len)
        st = [state_s[s, p] for s, p in chains]
        st_bf = [x.astype(BF16) for x in st]
        sg = [_dot(st_bf[n], bdiag(g_s[s * n_chunks + c, p].astype(BF16))) for n, (s, p) in enumerate(chains)]
        qs = [_dot_nt(q_s[s * n_chunks + c, p].astype(BF16), bdiag(st_bf[n])) for n, (s, p) in enumerate(chains)]
        for n, (s, p) in enumerate(chains):
            gc = s * n_chunks + c
            state_s[s, p] = st[n] * pc_s[gc][:, pair_cols(p)] - sg[n] + sa_s[gc, p]
            y_ref[0, s, rows, pair_cols(p)] = qs[n] + y0_s[gc, p]
        return carry

    lax.fori_loop(0, n_chunks, advance, 0, unroll=True)

    @pl.when(t == n_t - 1)
    def _():
        for s, p in chains:
            st = state_s[s, p]
            for j in range(HEADS_PER_PAIR):
                sout_ref[s, 0, HEADS_PER_PAIR * p + j] = st[:, j * HEAD_DIM:(j + 1) * HEAD_DIM]


def _rwkv_scan(pa, s0, wts, batch, seq_len):
    assert batch % SCAN_SEQS == 0, (batch, SCAN_SEQS)
    tile = min(SCAN_TILE, seq_len)
    n_t = seq_len // tile
    n_chunks = SCAN_SEQS * tile // SCAN_CHUNK

    def time_tile(d, t):
        return t + d * (n_t - 1 - 2 * t)

    per_dir = lambda shape: pl.BlockSpec((1,) + shape, lambda d, b, t: (d,) + (0,) * len(shape))
    state_spec = pl.BlockSpec((SCAN_SEQS, 1, N_HEADS_A, HEAD_DIM, HEAD_DIM), lambda d, b, t: (b, d, 0, 0, 0))
    pair_mat = pltpu.VMEM((n_chunks, N_PAIRS, SCAN_CHUNK, GROUP_LANES), F32)
    stream = pltpu.VMEM((SCAN_SEQS * tile, D_A), F32)
    token_spec = pl.BlockSpec((1, SCAN_SEQS, tile, D_A), lambda d, b, t: (d, b, time_tile(d, t), 0))
    y, s_new, kt = pl.pallas_call(
        functools.partial(_scan_kernel, n_t=n_t),
        grid=(2, batch // SCAN_SEQS, n_t),
        in_specs=[
            pl.BlockSpec((SCAN_SEQS, tile, COLS_A), lambda d, b, t: (b, time_tile(d, t), 0)), state_spec,
            per_dir((1, D_A)), per_dir((2 * R_DECAY, D_A)), per_dir((1, D_A)), per_dir((2 * R_ICLR, D_A)),
            per_dir((1, D_A)), per_dir((1, D_A)),
            _full((D_A, D_A)),
        ],
        out_specs=[token_spec, state_spec, token_spec],
        out_shape=[jax.ShapeDtypeStruct((2, batch, seq_len, D_A), F32),
                   jax.ShapeDtypeStruct((batch, 2, N_HEADS_A, HEAD_DIM, HEAD_DIM), F32),
                   jax.ShapeDtypeStruct((2, batch, seq_len, D_A), F32)],
        scratch_shapes=[
            stream, stream, stream, stream,
            pltpu.VMEM((SCAN_SEQS, N_PAIRS, HEAD_DIM, GROUP_LANES), F32),
            pair_mat, pair_mat, pair_mat, pair_mat,
            pltpu.VMEM((n_chunks, 1, D_A), F32),
        ],
        compiler_params=_params("arbitrary", "arbitrary", "arbitrary"),
        name="rwkv7_scan",
    )(pa.reshape(batch, seq_len, COLS_A), s0, wts["w0"], wts["w2_pad"], wts["a0"], wts["a2_pad"], wts["k_k"],
      wts["k_a"], wts["head_blocks"])
    return y.reshape(2, batch * seq_len, D_A), s_new, kt.reshape(2, batch * seq_len, D_A)


def _layernorm(x, g, b):
    mu = jnp.mean(x, axis=-1, keepdims=True)
    xc = x - mu
    var = jnp.mean(xc * xc, axis=-1, keepdims=True)
    return xc * lax.rsqrt(var + LN_EPS) * g + b


def _rmsnorm(x, g):
    return x * lax.rsqrt(jnp.mean(x * x, axis=-1, keepdims=True) + NORM_EPS) * g


def _mixer_kernel(pa_ref, pb_ref, pc_ref, yf_ref, yb_ref, ktf_ref, ktb_ref,
                  g2_ref, rk_ref, lnxg_ref, lnxb_ref, bd_ref,
                  gng_ref, gnb_ref, ws_ref, bs_ref, betab_ref,
                  cw_ref, cb_ref, cng_ref, cnb_ref, betac_ref,
                  cat_ref, cpad_ref, cphase_ref, *, conv_len):
    bd = bd_ref[...]

    r = pa_ref[:, 0:D_A]
    v = pa_ref[:, 2 * D_A:3 * D_A]
    gd = pa_ref[:, COL_GD:COL_GD + R_GATE]
    kt_sum = ktf_ref[0] + ktb_ref[0]
    bonus = _group_sum(r * kt_sum * rk_ref[...], bd) * v
    gate = _dot(_sigmoid(gd).astype(BF16), g2_ref[...])
    y = yf_ref[0] + yb_ref[0]
    mean = _group_sum(y, bd) * (1.0 / HEAD_DIM)
    yc = y - mean
    var = _group_sum(yc * yc, bd) * (1.0 / HEAD_DIM)
    yn = yc * lax.rsqrt(var + LN_X_EPS) * lnxg_ref[...] + lnxb_ref[...]
    cat_ref[:, 0:D_A] = ((yn + bonus) * gate).astype(BF16)

    gb = jax.nn.gelu(pb_ref[...])
    u = gb[:, :D_B]
    vg = _layernorm(gb[:, D_B:], gng_ref[...], gnb_ref[...]).astype(BF16)
    head_of_lane = lax.broadcasted_iota(jnp.int32, (GMLP_CHUNK, D_B), 1) // HEAD_DIM
    sv_chunks = []
    for n in range(TIME_TILE // GMLP_CHUNK):
        vgc = vg[n * GMLP_CHUNK:(n + 1) * GMLP_CHUNK]
        sv = bs_ref[...]
        for g in range(N_GROUPS_B):
            sv = sv + jnp.where(head_of_lane == g, _dot(ws_ref[g], vgc), 0.0)
        sv_chunks.append(sv)
    sv = jnp.concatenate(sv_chunks, axis=0)
    cat_ref[:, D_A:D_A + D_B] = _rmsnorm(u * sv, betab_ref[...]).astype(BF16)

    pc = pc_ref[...]
    gl = pc[:, :D_C] * _sigmoid(pc[:, D_C:])
    stride = conv_len + 2 * CONV_HALO
    zeros = jnp.zeros((CONV_HALO, D_C), F32)
    convs = []
    for q in range(TIME_TILE // conv_len):
        base = q * stride
        cpad_ref[base:base + CONV_HALO, :] = zeros
        cpad_ref[base + CONV_HALO:base + CONV_HALO + conv_len, :] = gl[q * conv_len:(q + 1) * conv_len]
        cpad_ref[base + CONV_HALO + conv_len:base + stride, :] = zeros
    total = (TIME_TILE // conv_len) * stride
    for s in range(1, SUBLANES):
        cphase_ref[s, 0:total - SUBLANES, :] = cpad_ref[s:s + total - SUBLANES, :]
    for q in range(TIME_TILE // conv_len):
        base = q * stride + CONV_HALO - CONV_PAD
        acc = jnp.zeros((conv_len, D_C), F32) + cb_ref[...]
        for j in range(CONV_K):
            phase = (base + j) % SUBLANES
            first = base + j - phase
            if phase == 0:
                taps = cpad_ref[first:first + conv_len, :]
            else:
                taps = cphase_ref[phase, first:first + conv_len, :]
            acc = acc + cw_ref[j:j + 1, :] * taps
        convs.append(acc)
    conv = jnp.concatenate(convs, axis=0) if len(convs) > 1 else convs[0]
    z = _layernorm(conv, cng_ref[...], cnb_ref[...])
    z = z * _sigmoid(z)
    cat_ref[:, D_A + D_B:] = _rmsnorm(z, betac_ref[...]).astype(BF16)


def _token_mixers(pa, pb, pc, y_scan, kt_scan, wts, conv_len):
    n = pa.shape[0]
    tile = lambda width: pl.BlockSpec((TIME_TILE, width), lambda i: (i, 0))
    y_dir = lambda d: pl.BlockSpec((1, TIME_TILE, D_A), lambda i: (d, i, 0))
    n_conv = TIME_TILE // conv_len
    return pl.pallas_call(
        functools.partial(_mixer_kernel, conv_len=conv_len),
        grid=(n // TIME_TILE,),
        in_specs=[
            tile(COLS_A), tile(COLS_B), tile(COLS_C), y_dir(0), y_dir(1), y_dir(0), y_dir(1),
            _full((R_GATE, D_A)), _full((1, D_A)), _full((1, D_A)), _full((1, D_A)), _full((D_A, D_A)),
            _full((1, D_B)), _full((1, D_B)), _full((N_GROUPS_B, GMLP_CHUNK, GMLP_CHUNK)),
            _full((GMLP_CHUNK, D_B)), _full((1, D_B)),
            _full((CONV_K, D_C)), _full((1, D_C)), _full((1, D_C)), _full((1, D_C)), _full((1, D_C)),
        ],
        out_specs=tile(D_MODEL),
        out_shape=jax.ShapeDtypeStruct((n, D_MODEL), BF16),
        scratch_shapes=[pltpu.VMEM((n_conv * (conv_len + 2 * CONV_HALO), D_C), F32),
                        pltpu.VMEM((SUBLANES, n_conv * (conv_len + 2 * CONV_HALO), D_C), F32)],
        compiler_params=_params("arbitrary"),
        name="token_mixers",
    )(pa, pb, pc, y_scan, y_scan, kt_scan, kt_scan,
      wts["g2"], wts["r_k"], wts["lnx_g"], wts["lnx_b"],
      wts["head_blocks"], wts["gmlp_norm_g"], wts["gmlp_norm_b"], wts["gmlp_ws"], wts["gmlp_bias"],
      wts["beta_b"], wts["conv_w"], wts["conv_b"], wts["conv_norm_g"], wts["conv_norm_b"], wts["beta_c"])


def _top2_sum(a, b, c, d):
    hi1, lo1 = jnp.maximum(a, b), jnp.minimum(a, b)
    hi2, lo2 = jnp.maximum(c, d), jnp.minimum(c, d)
    return jnp.maximum(hi1, hi2) + jnp.maximum(jnp.minimum(hi1, hi2), jnp.maximum(lo1, lo2))


def _router_gates(logits_t, bias_ref):
    m = jnp.max(logits_t, axis=0, keepdims=True)
    e = jnp.exp(logits_t - m)
    probs = e / jnp.sum(e, axis=0, keepdims=True)
    sel = probs + bias_ref[...]
    p_row = [probs[i:i + 1, :] for i in range(N_EXPERTS)]
    s_row = [sel[i:i + 1, :] for i in range(N_EXPERTS)]
    best_val = None
    best = None
    for g in range(N_EXPERT_GROUPS):
        score = _top2_sum(*s_row[g * EXPERTS_PER_GROUP:(g + 1) * EXPERTS_PER_GROUP])
        if g == 0:
            best_val, best = score, jnp.zeros_like(score, dtype=jnp.int32)
        else:
            better = score > best_val
            best_val = jnp.where(better, score, best_val)
            best = jnp.where(better, g, best)
    chosen = []
    for i in range(N_EXPERTS):
        g = i // EXPERTS_PER_GROUP
        rank = jnp.zeros_like(best)
        for j in range(g * EXPERTS_PER_GROUP, (g + 1) * EXPERTS_PER_GROUP):
            if j == i:
                continue
            ahead = (s_row[j] >= s_row[i]) if j < i else (s_row[j] > s_row[i])
            rank = rank + ahead.astype(jnp.int32)
        chosen.append((best == g) & (rank < 2))
    picked = [jnp.where(chosen[i], p_row[i], 0.0) for i in range(N_EXPERTS)]
    denom = picked[0]
    for i in range(1, N_EXPERTS):
        denom = denom + picked[i]
    gates = jnp.concatenate([p / denom for p in picked], axis=0)
    group_rows = [jnp.where(best == g, 1.0, 0.0) for g in range(N_EXPERT_GROUPS)]
    group_rows += [jnp.zeros_like(group_rows[0])] * (SUBLANES - N_EXPERT_GROUPS)
    return gates, jnp.concatenate(group_rows, axis=0)


def _outproj_kernel(cat_ref, x_ref, mod_ref, wout_ref, g_ref, wr_hi_ref, wr_lo_ref, br_ref,
                    x1_ref, h2_ref, gates_ref, group_ref):
    mod = mod_ref[0]
    out = _dot(cat_ref[...], wout_ref[...])
    x1 = x_ref[...] + mod[:, 2 * D_MODEL:3 * D_MODEL] * out
    x1_ref[...] = x1
    h2 = _modulated_rmsnorm(x1, g_ref[...], mod[:, 3 * D_MODEL:4 * D_MODEL], mod[:, 4 * D_MODEL:5 * D_MODEL])
    h_hi, h_lo = _split_bf16(h2, 2)
    h2_ref[...] = h_hi
    by_hi = _dot_nt(jnp.concatenate([wr_hi_ref[...], wr_lo_ref[...]], axis=0), h_hi)
    logits_t = by_hi[:N_EXPERTS] + _dot_nt(wr_hi_ref[...], h_lo) + by_hi[N_EXPERTS:]
    gates_ref[...], group_ref[...] = _router_gates(logits_t, br_ref)


def _out_projection(cat, x, mod, w_out, norm_g, wr_hi, wr_lo, b_router, seq_len, per_batch):
    n = x.shape[0]
    row = lambda width: pl.BlockSpec((ROW_TILE, width), lambda i: (i, 0))
    return pl.pallas_call(
        _outproj_kernel,
        grid=(n // ROW_TILE,),
        in_specs=[row(D_MODEL), row(D_MODEL), _mod_spec(seq_len, ROW_TILE, per_batch),
                  _full((D_MODEL, D_MODEL)), _full((1, D_MODEL)),
                  _full((N_EXPERTS, D_MODEL)), _full((N_EXPERTS, D_MODEL)), _full((N_EXPERTS, 1))],
        out_specs=[row(D_MODEL), row(D_MODEL), pl.BlockSpec((N_EXPERTS, ROW_TILE), lambda i: (0, i)),
                   pl.BlockSpec((SUBLANES, ROW_TILE), lambda i: (0, i))],
        out_shape=[jax.ShapeDtypeStruct((n, D_MODEL), F32), jax.ShapeDtypeStruct((n, D_MODEL), BF16),
                   jax.ShapeDtypeStruct((N_EXPERTS, n), F32), jax.ShapeDtypeStruct((SUBLANES, n), F32)],
        compiler_params=_params("arbitrary"),
        name="out_projection_router",
    )(cat, x, mod, w_out, norm_g, wr_hi, wr_lo, b_router)


def _moe_kernel(start_ref, h_ref, gates_ref, grp_t_ref, earlier_ref, x1_ref, mod_ref, wgu_ref,
                wdn_ref, fg_ref, o_ref, xs_ref, gs_ref, ys_ref, perm_ref, *, final_norm):
    blk = pl.program_id(0)
    grp = pl.program_id(1)
    tm = h_ref.shape[0]

    @pl.when(grp == 0)
    def _():
        earlier = earlier_ref[...]
        grp_t = grp_t_ref[...]
        rank_t = _dot(grp_t.astype(BF16), earlier)
        slot_t = jnp.zeros((1, tm), F32)
        for g in range(N_EXPERT_GROUPS):
            first = start_ref[blk, g].astype(F32)
            slot_t = slot_t + grp_t[g:g + 1, :] * (rank_t[g:g + 1, :] + first)
        slot_rows = lax.broadcasted_iota(jnp.int32, (tm, tm), 0)
        to_slots = jnp.where(slot_rows == slot_t.astype(jnp.int32), 1.0, 0.0).astype(BF16)
        perm_ref[...] = to_slots
        xs_ref[...] = _dot(to_slots, h_ref[...]).astype(BF16)
        pieces = _dot(to_slots, jnp.concatenate(_split_bf16(gates_ref[...], 3), axis=1))
        gs_ref[...] = (pieces[:, 0:N_EXPERTS] + pieces[:, N_EXPERTS:2 * N_EXPERTS]
                       + pieces[:, 2 * N_EXPERTS:3 * N_EXPERTS])
        ys_ref[...] = jnp.zeros_like(ys_ref)

    lo = start_ref[blk, grp] // MOE_CHUNK
    hi = (start_ref[blk, grp + 1] + MOE_CHUNK - 1) // MOE_CHUNK
    lane = lax.broadcasted_iota(jnp.int32, (MOE_CHUNK, N_EXPERTS), 1)

    def chunk(j, carry):
        rows = pl.ds(pl.multiple_of(j * MOE_CHUNK, MOE_CHUNK), MOE_CHUNK)
        x = xs_ref[rows, :]
        gates = gs_ref[rows, :]
        acts = []
        for e in range(EXPERTS_PER_GROUP):
            gate = jnp.sum(jnp.where(lane == grp * EXPERTS_PER_GROUP + e, gates, 0.0), axis=1, keepdims=True)
            gu = _dot(x, wgu_ref[0, e])
            g = gu[:, :D_EXPERT]
            acts.append((g * _sigmoid(g) * gu[:, D_EXPERT:] * gate).astype(BF16))
        ys_ref[rows, :] += _dot(jnp.concatenate(acts, axis=1), wdn_ref[0])
        return carry

    lax.fori_loop(lo, hi, chunk, 0)

    @pl.when(grp == N_EXPERT_GROUPS - 1)
    def _():
        moe = _dot_tn(perm_ref[...], ys_ref[...].astype(BF16))
        x2 = x1_ref[...] + mod_ref[0][:, 5 * D_MODEL:6 * D_MODEL] * moe
        if final_norm:
            x2 = _rmsnorm(x2, fg_ref[...])
        o_ref[...] = x2


def _mixture_of_experts(h2, gates_t, group_t, x1, mod, w_gu, w_down, earlier, final_g, seq_len, per_batch,
                        final_norm):
    n = x1.shape[0]
    n_blocks = n // MOE_ROW_TILE
    counts = group_t[:N_EXPERT_GROUPS].reshape(N_EXPERT_GROUPS, n_blocks, MOE_ROW_TILE).sum(axis=-1)
    starts = jnp.concatenate([jnp.zeros((1, n_blocks), F32), jnp.cumsum(counts, axis=0)], axis=0)
    starts = starts.T.astype(jnp.int32)
    row = lambda width: pl.BlockSpec((MOE_ROW_TILE, width), lambda i, g, s: (i, 0))
    col = lambda height: pl.BlockSpec((height, MOE_ROW_TILE), lambda i, g, s: (0, i))
    order = pl.BlockSpec((MOE_ROW_TILE, MOE_ROW_TILE), lambda i, g, s: (0, 0))
    mod_spec = _mod_spec(seq_len, MOE_ROW_TILE, per_batch)
    grid_spec = pltpu.PrefetchScalarGridSpec(
        num_scalar_prefetch=1,
        grid=(n_blocks, N_EXPERT_GROUPS),
        in_specs=[row(D_MODEL), row(N_EXPERTS), col(SUBLANES), order, row(D_MODEL), mod_spec,
                  pl.BlockSpec((1, EXPERTS_PER_GROUP, D_MODEL, 2 * D_EXPERT), lambda i, g, s: (g, 0, 0, 0)),
                  pl.BlockSpec((1, EXPERTS_PER_GROUP * D_EXPERT, D_MODEL), lambda i, g, s: (g, 0, 0)),
                  pl.BlockSpec((1, D_MODEL), lambda i, g, s: (0, 0))],
        out_specs=row(D_MODEL),
        scratch_shapes=[pltpu.VMEM((MOE_ROW_TILE, D_MODEL), BF16), pltpu.VMEM((MOE_ROW_TILE, N_EXPERTS), F32),
                        pltpu.VMEM((MOE_ROW_TILE, D_MODEL), F32), pltpu.VMEM((MOE_ROW_TILE, MOE_ROW_TILE), BF16)],
    )
    return pl.pallas_call(
        functools.partial(_moe_kernel, final_norm=final_norm),
        grid_spec=grid_spec,
        out_shape=jax.ShapeDtypeStruct((n, D_MODEL), F32),
        compiler_params=_params("arbitrary", "arbitrary"),
        name="mixture_of_experts",
    )(starts, h2, gates_t.T, group_t, earlier, x1, mod, w_gu, w_down, final_g)


def _pad_low_rank(w, rank):
    z = jnp.zeros_like(w[0])
    return jnp.stack([jnp.concatenate([w[0], z], axis=0), jnp.concatenate([z, w[1]], axis=0)]).astype(BF16)


def _layer_weights(l, head_blocks, mu_shift, w0, w2, a0, a2, k_k, k_a, g2, r_k, lnx_g, lnx_b, gmlp_norm_g,
                   gmlp_norm_b, gmlp_ws, gmlp_bs, beta_b, conv_w, conv_b, conv_norm_g, conv_norm_b, beta_c):
    return dict(
        mu_shift=mu_shift[l].reshape(1, COLS_A),
        w0=w0[l].reshape(2, 1, D_A), w2_pad=_pad_low_rank(w2[l], R_DECAY),
        a0=a0[l].reshape(2, 1, D_A), a2_pad=_pad_low_rank(a2[l], R_ICLR),
        k_k=k_k[l].reshape(2, 1, D_A), k_a=k_a[l].reshape(2, 1, D_A),
        g2=g2[l].astype(BF16), r_k=r_k[l].reshape(1, D_A),
        lnx_g=lnx_g[l].reshape(1, D_A), lnx_b=lnx_b[l].reshape(1, D_A), head_blocks=head_blocks,
        gmlp_norm_g=gmlp_norm_g[l].reshape(1, D_B), gmlp_norm_b=gmlp_norm_b[l].reshape(1, D_B),
        gmlp_ws=gmlp_ws[l].astype(BF16), gmlp_bias=jnp.repeat(gmlp_bs[l].T, HEAD_DIM, axis=1),
        beta_b=beta_b[l].reshape(1, D_B),
        conv_w=conv_w[l], conv_b=conv_b[l].reshape(1, D_C),
        conv_norm_g=conv_norm_g[l].reshape(1, D_C), conv_norm_b=conv_norm_b[l].reshape(1, D_C),
        beta_c=beta_c[l].reshape(1, D_C),
    )


def kernel(x_prompt, x_sample, state_rwkv, c, c_ctx, norm1_g, norm2_g, ada_w, ada_b, w_in, mu_shift, w0, w2, a0, a2, k_k, k_a, g2, r_k, lnx_g, lnx_b, gmlp_norm_g, gmlp_norm_b, gmlp_ws, gmlp_bs, beta_b, conv_w, conv_b, conv_norm_g, conv_norm_b, beta_c, w_out, w_router, b_router, moe_w_gu, moe_w_down, final_g):
    batch_p, seq_p, _ = x_prompt.shape
    batch_s, seq_s, _ = x_sample.shape

    n_cond = 1 + batch_s
    cond_rows = -(-n_cond // SUBLANES) * SUBLANES
    c_all = jnp.concatenate([c_ctx[None, :], c, jnp.zeros((cond_rows - n_cond, D_MODEL), F32)], axis=0)
    mod_all = _modulation(c_all, ada_w, ada_b)

    head_id = jnp.arange(D_A) // HEAD_DIM
    head_blocks = (head_id[:, None] == head_id[None, :]).astype(BF16)
    wr_t = w_router.T
    wr_hi = wr_t.astype(BF16)
    wr_lo = (wr_t - wr_hi.astype(F32)).astype(BF16)
    b_r = b_router.reshape(N_EXPERTS, 1)
    final_g2 = final_g.reshape(1, D_MODEL)
    layer_wts = [_layer_weights(l, head_blocks, mu_shift, w0, w2, a0, a2, k_k, k_a, g2, r_k, lnx_g, lnx_b,
                                gmlp_norm_g, gmlp_norm_b, gmlp_ws, gmlp_bs, beta_b, conv_w, conv_b, conv_norm_g,
                                conv_norm_b, beta_c) for l in range(DEPTH)]
    w_in_bf = [w_in[l].astype(BF16) for l in range(DEPTH)]
    w_out_bf = [w_out[l].astype(BF16) for l in range(DEPTH)]
    w_gu_bf = [moe_w_gu[l].astype(BF16).reshape(N_EXPERT_GROUPS, EXPERTS_PER_GROUP, D_MODEL, 2 * D_EXPERT)
               for l in range(DEPTH)]
    w_down_bf = [moe_w_down[l].astype(BF16).reshape(N_EXPERT_GROUPS, EXPERTS_PER_GROUP * D_EXPERT, D_MODEL)
                 for l in range(DEPTH)]
    token_id = jnp.arange(MOE_ROW_TILE)
    earlier = (token_id[:, None] < token_id[None, :]).astype(BF16)

    groups = [
        dict(x=x_prompt.reshape(batch_p * seq_p, D_MODEL), batch=batch_p, seq=seq_p, conv_len=seq_p,
             per_batch=False, s0=None, mod_rows=slice(0, 1)),
        dict(x=x_sample.reshape(batch_s * seq_s, D_MODEL), batch=batch_s, seq=seq_s, conv_len=GRID_W,
             per_batch=True, s0=state_rwkv, mod_rows=slice(1, 1 + batch_s)),
    ]
    ctx_states = []
    outputs = []
    for grp in groups:
        x = grp["x"]
        batch, seq, per_batch = grp["batch"], grp["seq"], grp["per_batch"]
        for l in range(DEPTH):
            mod = mod_all[l, grp["mod_rows"]][:, None, :]
            wts = layer_wts[l]
            pa, pb, pc = _in_projection(x, mod, norm1_g[l].reshape(1, D_MODEL), w_in_bf[l], wts["mu_shift"], seq,
                                        per_batch)
            if grp["s0"] is None:
                s0 = jnp.zeros((batch, 2, N_HEADS_A, HEAD_DIM, HEAD_DIM), F32)
            else:
                s0 = grp["s0"][:, l]
            y_scan, s_new, kt_scan = _rwkv_scan(pa, s0, wts, batch, seq)
            cat = _token_mixers(pa, pb, pc, y_scan, kt_scan, wts, grp["conv_len"])
            x1, h2, gates_t, group_t = _out_projection(cat, x, mod, w_out_bf[l], norm2_g[l].reshape(1, D_MODEL),
                                                       wr_hi, wr_lo, b_r, seq, per_batch)
            x = _mixture_of_experts(h2, gates_t, group_t, x1, mod, w_gu_bf[l], w_down_bf[l], earlier,
                                    final_g2, seq, per_batch, final_norm=(l == DEPTH - 1))
            if grp["s0"] is None:
                ctx_states.append(s_new)
        outputs.append(x.reshape(batch, seq, D_MODEL))
    return (outputs[0], outputs[1], jnp.stack(ctx_states, axis=1))
```
